```python
import math
import functools
import jax
import jax.numpy as jnp
from jax import lax
import numpy as np

D_MODEL = 2048
BATCH = 2
SEQ = 4096
DEPTH = 4
DEC_BATCH = 128
DEC_SEQ = 4
PAST_LEN = 8192
PAGE_SIZE = 128

N_MIXERS = 2
N_MLSTM = (DEPTH + 1) // 2
N_MLA = DEPTH // 2
EPS = 1e-6

MLSTM_HEADS = 8
MLSTM_DV = D_MODEL // MLSTM_HEADS
MLSTM_DK = MLSTM_DV // 2
MLSTM_QK = MLSTM_HEADS * MLSTM_DK
MLSTM_CHUNK = 64
GATE_SOFTCAP = 15.0
MLSTM_IN = 2 * MLSTM_QK + 2 * D_MODEL + 2 * MLSTM_HEADS

MLA_HEADS = 16
QK_NOPE = 128
QK_ROPE = 64
V_HEAD = 128
Q_LORA = 512
KV_LORA = 512
MLA_IN = Q_LORA + KV_LORA + QK_ROPE
ROPE_THETA = 10000.0
ATTN_SCALE = (QK_NOPE + QK_ROPE) ** -0.5
Q_BLOCK = 128

N_GROUPS = 8
EXPERTS_PER_GROUP = 8
N_EXPERTS = N_GROUPS * EXPERTS_PER_GROUP
TOP_K = 2
D_EXPERT = 512
MOE_BLOCK = 128

PLE_DIM = 256

kernel_name = 'hybrid_mlstm_mla_hmoe_step'


def rmsnorm(x, g):
    xf = x.astype(jnp.float32)
    y = xf * lax.rsqrt(jnp.mean(xf * xf, axis=-1, keepdims=True) + EPS)
    return (y * g.astype(jnp.float32)).astype(x.dtype)


def soft_cap(x):
    return GATE_SOFTCAP * jnp.tanh(x / GATE_SOFTCAP)


def rope_angles(pos):
    inv_freq = ROPE_THETA ** (-jnp.arange(0, QK_ROPE, 2, dtype=jnp.float32) / QK_ROPE)
    ang = pos.astype(jnp.float32)[:, None] * inv_freq[None, :]
    return jnp.cos(ang), jnp.sin(ang)


def apply_rope(x, cos, sin):
    x1, x2 = jnp.split(x.astype(jnp.float32), 2, axis=-1)
    return jnp.concatenate([x1 * cos - x2 * sin, x1 * sin + x2 * cos], axis=-1).astype(x.dtype)


def mlstm_chunk(state, q, k, v, logi, logf):
    C0, n0, m0 = state
    L = q.shape[2]
    b = jnp.cumsum(logf, axis=-1)
    causal = jnp.tril(jnp.ones((L, L), dtype=bool))
    d = jnp.where(causal, b[..., :, None] - b[..., None, :] + logi[..., None, :], -jnp.inf)
    carry_w = b + m0[..., None]
    m_row = jnp.maximum(carry_w, jnp.max(d, axis=-1))
    s = jnp.einsum('bhtd,bhsd->bhts', q, k) * jnp.exp(d - m_row[..., None])
    w_prev = jnp.exp(carry_w - m_row)
    num = jnp.einsum('bhts,bhsv->bhtv', s, v) + w_prev[..., None] * jnp.einsum('bhtd,bhdv->bhtv', q, C0)
    den = jnp.sum(s, axis=-1) + w_prev * jnp.einsum('bhtd,bhd->bht', q, n0)
    h = num / jnp.maximum(jnp.abs(den), jnp.exp(-m_row))[..., None]
    b_end = b[..., -1]
    d_end = b_end[..., None] - b + logi
    m_new = jnp.maximum(b_end + m0, jnp.max(d_end, axis=-1))
    a = jnp.exp(d_end - m_new[..., None])
    keep = jnp.exp(b_end + m0 - m_new)
    C_new = keep[..., None, None] * C0 + jnp.einsum('bhsd,bhsv->bhdv', a[..., None] * k, v)
    n_new = keep[..., None] * n0 + jnp.einsum('bhs,bhsd->bhd', a, k)
    return (C_new, n_new, m_new), h


def mlstm_scan(state0, q, k, v, logi, logf):
    B, H, T, _ = q.shape
    L = math.gcd(T, MLSTM_CHUNK)
    nc = T // L

    def chunks(a):
        return jnp.moveaxis(a.reshape((B, H, nc, L) + a.shape[3:]), 2, 0)

    def step(carry, xs):
        return mlstm_chunk(carry, *xs)

    state, hs = lax.scan(step, state0, (chunks(q), chunks(k), chunks(v), chunks(logi), chunks(logf)))
    return state, jnp.moveaxis(hs, 0, 2).reshape(B, H, T, -1)


def mlstm_mixer(u, state0, w_in, b_if, g_head, w_out):
    B, T, _ = u.shape
    proj = u @ w_in
    q, k, v, o, gates = jnp.split(
        proj, [MLSTM_QK, 2 * MLSTM_QK, 2 * MLSTM_QK + D_MODEL, 2 * MLSTM_QK + 2 * D_MODEL], axis=-1)
    gates = soft_cap(gates.astype(jnp.float32) + b_if.astype(jnp.float32))
    logi, f_pre = jnp.split(gates.transpose(0, 2, 1), 2, axis=1)
    logf = jax.nn.log_sigmoid(f_pre)

    def heads(a, dh):
        return a.reshape(B, T, MLSTM_HEADS, dh).transpose(0, 2, 1, 3).astype(jnp.float32)

    q = heads(q, MLSTM_DK) * (MLSTM_DK ** -0.5)
    k = heads(k, MLSTM_DK)
    v = heads(v, MLSTM_DV)
    state0 = (state0[0].astype(jnp.float32), state0[1].astype(jnp.float32), state0[2].astype(jnp.float32))
    state, h = mlstm_scan(state0, q, k, v, logi, logf)
    h = rmsnorm(h.transpose(0, 2, 1, 3), g_head).reshape(B, T, D_MODEL)
    out = (jax.nn.sigmoid(o.astype(jnp.float32)) * h).astype(u.dtype) @ w_out
    return out, state


def mla_causal_attention(q_lat, q_rope, ckv, kr):
    B, T = q_lat.shape[:2]
    qb = math.gcd(T, Q_BLOCK)
    key_pos = jnp.arange(T)

    def block(i):
        ql = lax.dynamic_slice_in_dim(q_lat, i * qb, qb, axis=1)
        qr = lax.dynamic_slice_in_dim(q_rope, i * qb, qb, axis=1)
        s = (jnp.einsum('bthc,bsc->bhts', ql, ckv) + jnp.einsum('bthr,bsr->bhts', qr, kr)).astype(jnp.float32)
        q_pos = i * qb + jnp.arange(qb)
        s = jnp.where(key_pos[None, :] <= q_pos[:, None], s * ATTN_SCALE, -jnp.inf)
        p = jax.nn.softmax(s, axis=-1).astype(ckv.dtype)
        return jnp.einsum('bhts,bsc->bthc', p, ckv)

    o = lax.map(block, jnp.arange(T // qb))
    return jnp.moveaxis(o, 0, 1).reshape(B, T, MLA_HEADS, KV_LORA)


def mla_paged_attention(q_lat, q_rope, ckv, kr, ckv_pool, kr_pool, page_table):
    B, T = q_lat.shape[:2]
    past_ckv = ckv_pool[page_table].reshape(B, -1, KV_LORA)
    past_kr = kr_pool[page_table].reshape(B, -1, QK_ROPE)
    P = past_ckv.shape[1]
    s_past = (jnp.einsum('bthc,bsc->bhts', q_lat, past_ckv)
              + jnp.einsum('bthr,bsr->bhts', q_rope, past_kr)).astype(jnp.float32)
    s_new = (jnp.einsum('bthc,bsc->bhts', q_lat, ckv) + jnp.einsum('bthr,bsr->bhts', q_rope, kr)).astype(jnp.float32)
    causal = jnp.tril(jnp.ones((T, T), dtype=bool))
    s_new = jnp.where(causal, s_new, -jnp.inf)
    p = jax.nn.softmax(jnp.concatenate([s_past, s_new], axis=-1) * ATTN_SCALE, axis=-1).astype(ckv.dtype)
    return (jnp.einsum('bhts,bsc->bthc', p[..., :P], past_ckv)
            + jnp.einsum('bhts,bsc->bthc', p[..., P:], ckv))


def mla_mixer(u, pos, attend, w_a, g_q, w_qb, g_kv, w_kvb, w_o):
    B, T, _ = u.shape
    a = u @ w_a
    cq, ckv, kr = jnp.split(a, [Q_LORA, Q_LORA + KV_LORA], axis=-1)
    q = (rmsnorm(cq, g_q) @ w_qb).reshape(B, T, MLA_HEADS, QK_NOPE + QK_ROPE)
    q_nope, q_rope = jnp.split(q, [QK_NOPE], axis=-1)
    cos, sin = rope_angles(pos)
    q_rope = apply_rope(q_rope, cos[:, None, :], sin[:, None, :])
    ckv = rmsnorm(ckv, g_kv)
    kr = apply_rope(kr, cos, sin)
    wk, wv = jnp.split(w_kvb.reshape(KV_LORA, MLA_HEADS, QK_NOPE + V_HEAD), [QK_NOPE], axis=-1)
    q_lat = jnp.einsum('bthn,chn->bthc', q_nope, wk)
    o_lat = attend(q_lat, q_rope, ckv, kr)
    o = jnp.einsum('bthc,chv->bthv', o_lat, wv).reshape(B, T, MLA_HEADS * V_HEAD) @ w_o
    return o, (ckv, kr)


def hier_route(u, w_rg, b_rg, w_re, b_re):
    N = u.shape[0]
    pg = jax.nn.softmax((u @ w_rg).astype(jnp.float32) + b_rg.astype(jnp.float32), axis=-1)
    g_sel = jnp.argmax(pg, axis=-1)
    pg_sel = jnp.max(pg, axis=-1)
    le = ((u @ w_re).astype(jnp.float32) + b_re.astype(jnp.float32)).reshape(N, N_GROUPS, EXPERTS_PER_GROUP)
    pe = jax.nn.softmax(le[jnp.arange(N), g_sel], axis=-1)
    w_top, e_top = lax.top_k(pe, TOP_K)
    gates = pg_sel[:, None] * w_top / jnp.sum(w_top, axis=-1, keepdims=True)
    experts = (g_sel[:, None] * EXPERTS_PER_GROUP + e_top).astype(jnp.int32)
    return experts, gates


def moe_ffn(u, experts, gates, w1, w3, w2):
    N, D = u.shape
    A = N * TOP_K
    n_blocks = -(-A // MOE_BLOCK) + N_EXPERTS
    flat_e = experts.reshape(A)
    order = jnp.argsort(flat_e)
    sorted_e = flat_e[order]
    counts = jnp.bincount(flat_e, length=N_EXPERTS)
    padded = (counts + MOE_BLOCK - 1) // MOE_BLOCK * MOE_BLOCK
    start = jnp.cumsum(counts) - counts
    pad_end = jnp.cumsum(padded)
    pad_start = pad_end - padded
    slot_sorted = (pad_start[sorted_e] + jnp.arange(A) - start[sorted_e]).astype(jnp.int32)
    slot = jnp.zeros((A,), jnp.int32).at[order].set(slot_sorted)
    token_of_slot = jnp.full((n_blocks * MOE_BLOCK,), N, jnp.int32).at[slot].set(
        jnp.arange(A, dtype=jnp.int32) // TOP_K)
    u_pad = jnp.concatenate([u, jnp.zeros((1, D), u.dtype)], axis=0)[token_of_slot].reshape(n_blocks, MOE_BLOCK, D)
    block_expert = jnp.minimum(
        jnp.searchsorted(pad_end, jnp.arange(n_blocks) * MOE_BLOCK, side='right'), N_EXPERTS - 1)

    def expert_block(args):
        xb, e = args
        return (jax.nn.silu(xb @ w1[e]) * (xb @ w3[e])) @ w2[e]

    y_pad = lax.map(expert_block, (u_pad, block_expert)).reshape(-1, D)
    y = y_pad[slot].reshape(N, TOP_K, D)
    return jnp.einsum('nkd,nk->nd', y, gates.astype(u.dtype))


def per_layer_embed(h, p_i, g, w_gate, w_proj):
    gate = jax.nn.sigmoid((rmsnorm(h, g) @ w_gate).astype(jnp.float32))
    return h + (gate * (p_i @ w_proj).astype(jnp.float32)).astype(h.dtype)


def trunk(x, p, pos, mlstm_state0, mla_attend, w):
    B, T, D = x.shape
    h = x
    ckv_rows, kr_rows, C_out, n_out, m_out = [], [], [], [], []
    for i in range(DEPTH):
        j = i // N_MIXERS
        u = rmsnorm(h, w['ln_mix'][i])
        if i % N_MIXERS == 0:
            st0 = (mlstm_state0[0][j], mlstm_state0[1][j], mlstm_state0[2][j])
            mix, (C, n, m) = mlstm_mixer(u, st0, w['mlstm_w_in'][j], w['mlstm_b_if'][j],
                                         w['mlstm_g_head'][j], w['mlstm_w_out'][j])
            C_out.append(C)
            n_out.append(n)
            m_out.append(m)
        else:
            mix, (ckv, kr) = mla_mixer(u, pos, functools.partial(mla_attend, j), w['mla_w_a'][j], w['mla_g_q'][j],
                                       w['mla_w_qb'][j], w['mla_g_kv'][j], w['mla_w_kvb'][j], w['mla_w_o'][j])
            ckv_rows.append(ckv)
            kr_rows.append(kr)
        h = h + mix
        u = rmsnorm(h, w['ln_ffn'][i]).reshape(B * T, D)
        experts, gates = hier_route(u, w['moe_w_rg'][i], w['moe_b_rg'][i], w['moe_w_re'][i], w['moe_b_re'][i])
        h = h + moe_ffn(u, experts, gates, w['moe_w1'][i], w['moe_w3'][i], w['moe_w2'][i]).reshape(B, T, D)
        h = per_layer_embed(h, p[i], w['ple_g'][i], w['ple_w_gate'][i], w['ple_w_proj'][i])
    y = rmsnorm(h, w['ln_final'])
    return y, (jnp.stack(ckv_rows), jnp.stack(kr_rows)), (jnp.stack(C_out), jnp.stack(n_out), jnp.stack(m_out))


def setup_inputs(seed: int = 0) -> dict:
    key = jax.random.key(seed)
    keys = iter(jax.random.split(key, 48))
    f32 = jnp.float32

    def normal(shape, scale=1.0):
        return jax.random.normal(next(keys), shape, f32) * scale

    def gain(shape):
        return 1.0 + 0.05 * normal(shape)

    n_pages = PAST_LEN // PAGE_SIZE
    n_used = DEC_BATCH * n_pages
    n_pool = -(-5 * n_used // 4)
    d = D_MODEL
    inputs = {}
    inputs['x_prompt'] = normal((BATCH, SEQ, d))
    inputs['x_sample'] = normal((DEC_BATCH, DEC_SEQ, d))
    inputs['cache_mla_ckv'] = normal((N_MLA, n_pool, PAGE_SIZE, KV_LORA))
    inputs['cache_mla_krope'] = normal((N_MLA, n_pool, PAGE_SIZE, QK_ROPE))
    inputs['state_mlstm_C'] = normal((N_MLSTM, DEC_BATCH, MLSTM_HEADS, MLSTM_DK, MLSTM_DV))
    inputs['state_mlstm_n'] = normal((N_MLSTM, DEC_BATCH, MLSTM_HEADS, MLSTM_DK))
    inputs['state_mlstm_m'] = normal((N_MLSTM, DEC_BATCH, MLSTM_HEADS))
    inputs['page_table'] = jax.random.permutation(next(keys), n_pool)[:n_used].reshape(
        DEC_BATCH, n_pages).astype(jnp.int32)
    inputs['p_prompt'] = normal((DEPTH, BATCH, SEQ, PLE_DIM))
    inputs['p_sample'] = normal((DEPTH, DEC_BATCH, DEC_SEQ, PLE_DIM))
    inputs['ln_mix'] = gain((DEPTH, d))
    inputs['ln_ffn'] = gain((DEPTH, d))
    inputs['ln_final'] = gain((d,))
    inputs['mlstm_w_in'] = normal((N_MLSTM, d, MLSTM_IN), d ** -0.5)
    inputs['mlstm_b_if'] = jnp.concatenate(
        [0.1 * normal((N_MLSTM, MLSTM_HEADS)), 3.0 + 0.1 * normal((N_MLSTM, MLSTM_HEADS))], axis=-1)
    inputs['mlstm_g_head'] = gain((N_MLSTM, MLSTM_HEADS, MLSTM_DV))
    inputs['mlstm_w_out'] = normal((N_MLSTM, d, d), d ** -0.5)
    inputs['mla_w_a'] = normal((N_MLA, d, MLA_IN), d ** -0.5)
    inputs['mla_g_q'] = gain((N_MLA, Q_LORA))
    inputs['mla_w_qb'] = normal((N_MLA, Q_LORA, MLA_HEADS * (QK_NOPE + QK_ROPE)), Q_LORA ** -0.5)
    inputs['mla_g_kv'] = gain((N_MLA, KV_LORA))
    inputs['mla_w_kvb'] = normal((N_MLA, KV_LORA, MLA_HEADS * (QK_NOPE + V_HEAD)), KV_LORA ** -0.5)
    inputs['mla_w_o'] = normal((N_MLA, MLA_HEADS * V_HEAD, d), (MLA_HEADS * V_HEAD) ** -0.5)
    inputs['moe_w_rg'] = normal((DEPTH, d, N_GROUPS), d ** -0.5)
    inputs['moe_b_rg'] = normal((DEPTH, N_GROUPS), 0.01)
    inputs['moe_w_re'] = normal((DEPTH, d, N_EXPERTS), d ** -0.5)
    inputs['moe_b_re'] = normal((DEPTH, N_EXPERTS), 0.01)
    inputs['moe_w1'] = normal((DEPTH, N_EXPERTS, d, D_EXPERT), d ** -0.5)
    inputs['moe_w3'] = normal((DEPTH, N_EXPERTS, d, D_EXPERT), d ** -0.5)
    inputs['moe_w2'] = normal((DEPTH, N_EXPERTS, D_EXPERT, d), D_EXPERT ** -0.5)
    inputs['ple_g'] = gain((DEPTH, d))
    inputs['ple_w_gate'] = normal((DEPTH, d, d), d ** -0.5)
    inputs['ple_w_proj'] = normal((DEPTH, PLE_DIM, d), PLE_DIM ** -0.5)
    return inputs


def reference(x_prompt, x_sample, cache_mla_ckv, cache_mla_krope, state_mlstm_C, state_mlstm_n, state_mlstm_m,
              page_table, p_prompt, p_sample, ln_mix, ln_ffn, ln_final, mlstm_w_in, mlstm_b_if, mlstm_g_head,
              mlstm_w_out, mla_w_a, mla_g_q, mla_w_qb, mla_g_kv, mla_w_kvb, mla_w_o, moe_w_rg, moe_b_rg, moe_w_re,
              moe_b_re, moe_w1, moe_w3, moe_w2, ple_g, ple_w_gate, ple_w_proj):
    w = dict(ln_mix=ln_mix, ln_ffn=ln_ffn, ln_final=ln_final, mlstm_w_in=mlstm_w_in, mlstm_b_if=mlstm_b_if,
             mlstm_g_head=mlstm_g_head, mlstm_w_out=mlstm_w_out, mla_w_a=mla_w_a, mla_g_q=mla_g_q,
             mla_w_qb=mla_w_qb, mla_g_kv=mla_g_kv, mla_w_kvb=mla_w_kvb, mla_w_o=mla_w_o, moe_w_rg=moe_w_rg,
             moe_b_rg=moe_b_rg, moe_w_re=moe_w_re, moe_b_re=moe_b_re, moe_w1=moe_w1, moe_w3=moe_w3,
             moe_w2=moe_w2, ple_g=ple_g, ple_w_gate=ple_w_gate, ple_w_proj=ple_w_proj)

    B, T, _ = x_prompt.shape
    zero_state = (jnp.zeros((N_MLSTM, B, MLSTM_HEADS, MLSTM_DK, MLSTM_DV), jnp.float32),
                  jnp.zeros((N_MLSTM, B, MLSTM_HEADS, MLSTM_DK), jnp.float32),
                  jnp.zeros((N_MLSTM, B, MLSTM_HEADS), jnp.float32))

    def prompt_attend(j, q_lat, q_rope, ckv, kr):
        return mla_causal_attention(q_lat, q_rope, ckv, kr)

    y_prompt, (ckv_p, kr_p), (C_p, n_p, m_p) = trunk(x_prompt, p_prompt, jnp.arange(T), zero_state, prompt_attend, w)

    past_len = page_table.shape[1] * PAGE_SIZE
    Ts = x_sample.shape[1]

    def sample_attend(j, q_lat, q_rope, ckv, kr):
        return mla_paged_attention(q_lat, q_rope, ckv, kr, cache_mla_ckv[j], cache_mla_krope[j], page_table)

    y_sample, (ckv_s, kr_s), (C_s, n_s, m_s) = trunk(
        x_sample, p_sample, past_len + jnp.arange(Ts), (state_mlstm_C, state_mlstm_n, state_mlstm_m), sample_attend, w)

    return (y_prompt, y_sample, ckv_p, kr_p, C_p, n_p, m_p, ckv_s, kr_s, C_s, n_s, m_s)
```

```python
import functools

import jax
import jax.numpy as jnp
from jax import lax
from jax.experimental import pallas as pl
from jax.experimental.pallas import tpu as pltpu

F32 = jnp.float32
BF16 = jnp.bfloat16
HIGHEST = lax.Precision.HIGHEST

D_MODEL = 2048
BATCH = 2
SEQ = 4096
DEPTH = 4
DEC_BATCH = 128
DEC_SEQ = 4
PAST_LEN = 8192
PAGE_SIZE = 128
N_PAGES = PAST_LEN // PAGE_SIZE
EPS = 1e-6

MLSTM_HEADS = 8
MLSTM_DV = D_MODEL // MLSTM_HEADS
MLSTM_DK = MLSTM_DV // 2
MLSTM_QK = MLSTM_HEADS * MLSTM_DK
GATE_SOFTCAP = 15.0
MLSTM_MAIN = 2 * MLSTM_QK + 2 * D_MODEL
QKV_TILES = (2 * MLSTM_QK + D_MODEL) // MLSTM_QK

MLA_HEADS = 16
QK_NOPE = 128
QK_ROPE = 64
V_HEAD = 128
Q_LORA = 512
KV_LORA = 512
ROPE_THETA = 10000.0
ATTN_SCALE = (QK_NOPE + QK_ROPE) ** -0.5
QK_WIDE = KV_LORA + 128

N_GROUPS = 8
EXPERTS_PER_GROUP = 8
N_EXPERTS = N_GROUPS * EXPERTS_PER_GROUP
TOP_K = 2
D_EXPERT = 512
PLE_DIM = 256

N_PROMPT = BATCH * SEQ
N_SAMPLE = DEC_BATCH * DEC_SEQ
N_TOK = N_PROMPT + N_SAMPLE

LANES = 128
NEG = -1e30
VMEM_LIMIT = 56 * 1024 * 1024

TM = 512
TM_ROW = 256
MLSTM_CHUNK_P = 64
MLSTM_CHUNK_S = 16
TQ = 256
FLASH_ROWS = 1024
PAGES_PER_STEP = 8
MOE_BM = 256
GATHER_ROWS = 256


def _cparams(sem):
    return pltpu.CompilerParams(dimension_semantics=sem, vmem_limit_bytes=VMEM_LIMIT)


def _rms(x, g):
    return x * lax.rsqrt(jnp.mean(x * x, axis=-1, keepdims=True) + EPS) * g


def _sigmoid(x):
    return 1.0 / (1.0 + jnp.exp(-x))


def _round_bf16(x):
    return x.astype(BF16).astype(F32)


def _norm_proj_body(h_ref, g_ref, w_ref, wg_ref, qkv_ref, o_ref, gate_ref, xn_ref):
    j = pl.program_id(1)

    @pl.when(j == 0)
    def _():
        xn = _rms(h_ref[...], g_ref[...]).astype(BF16)
        xn_ref[...] = xn
        gate_ref[...] = jnp.dot(xn, wg_ref[...], preferred_element_type=F32)

    acc = jnp.dot(xn_ref[...], w_ref[...], preferred_element_type=F32)

    @pl.when(j == 0)
    def _():
        qkv_ref[...] = (acc * MLSTM_DK ** -0.5).astype(BF16)

    @pl.when(jnp.logical_and(j > 0, j < QKV_TILES))
    def _():
        qkv_ref[...] = acc.astype(BF16)

    @pl.when(j >= QKV_TILES)
    def _():
        o_ref[...] = acc


def norm_proj(h, g, w, wg):
    n, d = h.shape
    tn = MLSTM_QK
    return pl.pallas_call(
        _norm_proj_body,
        grid=(n // TM, MLSTM_MAIN // tn),
        in_specs=[
            pl.BlockSpec((TM, d), lambda i, j: (i, 0)),
            pl.BlockSpec((1, d), lambda i, j: (0, 0)),
            pl.BlockSpec((d, tn), lambda i, j: (0, j)),
            pl.BlockSpec((d, LANES), lambda i, j: (0, 0)),
        ],
        out_specs=[
            pl.BlockSpec((TM, tn), lambda i, j: (i, jnp.minimum(j, QKV_TILES - 1))),
            pl.BlockSpec((TM, tn), lambda i, j: (i, jnp.maximum(j - QKV_TILES, 0))),
            pl.BlockSpec((TM, LANES), lambda i, j: (i, 0)),
        ],
        out_shape=[
            jax.ShapeDtypeStruct((n, 2 * MLSTM_QK + D_MODEL), BF16),
            jax.ShapeDtypeStruct((n, D_MODEL), F32),
            jax.ShapeDtypeStruct((n, LANES), F32),
        ],
        scratch_shapes=[pltpu.VMEM((TM, d), BF16)],
        compiler_params=_cparams(("parallel", "arbitrary")),
        name="norm_proj",
    )(h, g, w, wg)


def _linear_res_body(x_ref, w_ref, r_ref, o_ref):
    o_ref[...] = r_ref[...] + jnp.dot(x_ref[...], w_ref[...], preferred_element_type=F32)


def linear_res(h, x, w, *, row_block_offset, tn=1024):
    m, k = x.shape
    d = w.shape[1]
    return pl.pallas_call(
        _linear_res_body,
        grid=(m // TM, d // tn),
        in_specs=[
            pl.BlockSpec((TM, k), lambda i, j: (i, 0)),
            pl.BlockSpec((k, tn), lambda i, j: (0, j)),
            pl.BlockSpec((TM, tn), lambda i, j: (i + row_block_offset, j)),
        ],
        out_specs=pl.BlockSpec((TM, tn), lambda i, j: (i + row_block_offset, j)),
        out_shape=jax.ShapeDtypeStruct(h.shape, F32),
        input_output_aliases={2: 0},
        compiler_params=_cparams(("parallel", "parallel")),
        name="linear_res",
    )(x, w, h)


def _mlstm_body(q_ref, k_ref, v_ref, o_ref, gt_ref, bif_ref, gh_ref, c0_ref, n0_ref, m0_ref,
                out_ref, c_ref, n_ref, m_ref, *, chunk, valid):
    L = chunk

    @pl.when(pl.program_id(1) == 0)
    def _():
        c_ref[...] = c0_ref[...]
        n_ref[...] = n0_ref[...]
        m_ref[...] = m0_ref[...]

    gc = GATE_SOFTCAP * jnp.tanh((gt_ref[...] + bif_ref[...]) / GATE_SOFTCAP)
    logf_all = jnp.minimum(gc, 0.0) - jnp.log1p(jnp.exp(-jnp.abs(gc)))
    logi_all = gc
    if valid < L:
        live = lax.broadcasted_iota(jnp.int32, (L, 1), 0) < valid
        logi_all = jnp.where(live, gc, NEG)
        logf_all = jnp.where(live, logf_all, 0.0)
    rows = lax.broadcasted_iota(jnp.int32, (L, L), 0)
    cols = lax.broadcasted_iota(jnp.int32, (L, L), 1)
    causal = rows >= cols
    eye = rows == cols
    b_all = jnp.dot(causal.astype(F32), logf_all, precision=HIGHEST, preferred_element_type=F32)

    def as_row(col):
        return jnp.sum(jnp.where(eye, col, 0.0), axis=0, keepdims=True)

    for hd in range(MLSTM_HEADS):
        b_col = b_all[:, MLSTM_HEADS + hd:MLSTM_HEADS + hd + 1]
        li_col = logi_all[:, hd:hd + 1]
        b_row = as_row(b_col)
        li_row = as_row(li_col)
        m0 = m_ref[0, :, hd:hd + 1]
        d = jnp.where(causal, b_col - b_row + li_row, NEG)
        carry = b_col + m0
        m_row = jnp.maximum(carry, jnp.max(d, axis=-1, keepdims=True))
        q = q_ref[:, hd * MLSTM_DK:(hd + 1) * MLSTM_DK]
        k = k_ref[:, hd * MLSTM_DK:(hd + 1) * MLSTM_DK]
        v = v_ref[:, hd * MLSTM_DV:(hd + 1) * MLSTM_DV]
        qk = lax.dot_general(q, k, (((1,), (1,)), ((), ())), preferred_element_type=F32)
        s = qk * jnp.exp(d - m_row)
        w_prev = jnp.exp(carry - m_row)
        c_old = c_ref[0, hd]
        n_old = n_ref[0, hd:hd + 1, :]
        num = (jnp.dot(s.astype(BF16), v, preferred_element_type=F32)
               + w_prev * jnp.dot(q, c_old.astype(BF16), preferred_element_type=F32))
        den = (jnp.sum(s, axis=-1, keepdims=True)
               + w_prev * jnp.sum(q.astype(F32) * _round_bf16(n_old), axis=-1, keepdims=True))
        hh = num / jnp.maximum(jnp.abs(den), jnp.exp(-m_row))

        b_end = b_col[L - 1:L, :]
        dend = b_end - b_col + li_col
        m_new = jnp.maximum(b_end + m0, jnp.max(dend, axis=0, keepdims=True))
        a_col = jnp.exp(dend - m_new)
        keep = jnp.exp(b_end + m0 - m_new)
        kf = k.astype(F32)
        upd = lax.dot_general((a_col * kf).astype(BF16), v, (((0,), (0,)), ((), ())), preferred_element_type=F32)
        c_ref[0, hd] = keep * c_old + upd
        n_ref[0, hd:hd + 1, :] = keep * n_old + jnp.sum(_round_bf16(a_col) * kf, axis=0, keepdims=True)
        m_ref[0, :, hd:hd + 1] = m_new

        sl = slice(hd * MLSTM_DV, (hd + 1) * MLSTM_DV)
        hn = hh * lax.rsqrt(jnp.mean(hh * hh, axis=-1, keepdims=True) + EPS) * gh_ref[:, sl]
        out_ref[:, sl] = (_sigmoid(o_ref[:, sl]) * hn).astype(BF16)


def mlstm_scan(qkv, o_gate, gates, b_if, g_head, c0, n0, m0, *, n_seq, n_chunks, chunk, valid):
    rows = n_seq * n_chunks * chunk
    row = lambda s, c: s * n_chunks + c
    body = functools.partial(_mlstm_body, chunk=chunk, valid=valid)
    return pl.pallas_call(
        body,
        grid=(n_seq, n_chunks),
        in_specs=[
            pl.BlockSpec((chunk, MLSTM_QK), lambda s, c: (row(s, c), 0)),
            pl.BlockSpec((chunk, MLSTM_QK), lambda s, c: (row(s, c), 1)),
            pl.BlockSpec((chunk, D_MODEL), lambda s, c: (row(s, c), 1)),
            pl.BlockSpec((chunk, D_MODEL), lambda s, c: (row(s, c), 0)),
            pl.BlockSpec((chunk, LANES), lambda s, c: (row(s, c), 0)),
            pl.BlockSpec((1, LANES), lambda s, c: (0, 0)),
            pl.BlockSpec((1, D_MODEL), lambda s, c: (0, 0)),
            pl.BlockSpec((1, MLSTM_HEADS, MLSTM_DK, MLSTM_DV), lambda s, c: (s, 0, 0, 0)),
            pl.BlockSpec((1, MLSTM_HEADS, MLSTM_DK), lambda s, c: (s, 0, 0)),
            pl.BlockSpec((1, 1, LANES), lambda s, c: (s, 0, 0)),
        ],
        out_specs=[
            pl.BlockSpec((chunk, D_MODEL), lambda s, c: (row(s, c), 0)),
            pl.BlockSpec((1, MLSTM_HEADS, MLSTM_DK, MLSTM_DV), lambda s, c: (s, 0, 0, 0)),
            pl.BlockSpec((1, MLSTM_HEADS, MLSTM_DK), lambda s, c: (s, 0, 0)),
            pl.BlockSpec((1, 1, LANES), lambda s, c: (s, 0, 0)),
        ],
        out_shape=[
            jax.ShapeDtypeStruct((rows, D_MODEL), BF16),
            jax.ShapeDtypeStruct((n_seq, MLSTM_HEADS, MLSTM_DK, MLSTM_DV), F32),
            jax.ShapeDtypeStruct((n_seq, MLSTM_HEADS, MLSTM_DK), F32),
            jax.ShapeDtypeStruct((n_seq, 1, LANES), F32),
        ],
        compiler_params=_cparams(("parallel", "arbitrary")),
        name="mlstm_scan",
    )(qkv, qkv, qkv, o_gate, gates, b_if, g_head, c0, n0, m0)


def _mla_a_body(h_ref, g_ref, w_ref, gq_ref, gkv_ref, cc_ref, ss_ref, cq_ref, ckv_ref, kr_ref, kw_ref):
    xn = _rms(h_ref[...], g_ref[...]).astype(BF16)
    a = jnp.dot(xn, w_ref[...], preferred_element_type=F32)
    cq_ref[...] = _rms(a[:, :Q_LORA], gq_ref[...]).astype(BF16)
    ckv = _rms(a[:, Q_LORA:Q_LORA + KV_LORA], gkv_ref[...])
    ckv_ref[...] = ckv
    t = a[:, Q_LORA + KV_LORA:]
    kr = t[:, :QK_ROPE] * cc_ref[:, :QK_ROPE] + t[:, QK_ROPE:] * ss_ref[:, :QK_ROPE]
    kr_ref[...] = kr
    kw_ref[:, :KV_LORA] = ckv.astype(BF16)
    kw_ref[:, KV_LORA:] = jnp.concatenate([kr, jnp.zeros_like(kr)], axis=1).astype(BF16)


def mla_a(h, g, w, gq, gkv, cc, ss):
    n, d = h.shape
    wa = w.shape[1]
    tm = TM_ROW
    row = lambda i: (i, 0)
    fixed = lambda i: (0, 0)
    return pl.pallas_call(
        _mla_a_body,
        grid=(n // tm,),
        in_specs=[
            pl.BlockSpec((tm, d), row),
            pl.BlockSpec((1, d), fixed),
            pl.BlockSpec((d, wa), fixed),
            pl.BlockSpec((1, Q_LORA), fixed),
            pl.BlockSpec((1, KV_LORA), fixed),
            pl.BlockSpec((tm, LANES), row),
            pl.BlockSpec((tm, LANES), row),
        ],
        out_specs=[
            pl.BlockSpec((tm, Q_LORA), row),
            pl.BlockSpec((tm, KV_LORA), row),
            pl.BlockSpec((tm, QK_ROPE), row),
            pl.BlockSpec((tm, QK_WIDE), row),
        ],
        out_shape=[
            jax.ShapeDtypeStruct((n, Q_LORA), BF16),
            jax.ShapeDtypeStruct((n, KV_LORA), F32),
            jax.ShapeDtypeStruct((n, QK_ROPE), F32),
            jax.ShapeDtypeStruct((n, QK_WIDE), BF16),
        ],
        compiler_params=_cparams(("parallel",)),
        name="mla_a",
    )(h, g, w, gq, gkv, cc, ss)


def _mla_q_body(c_ref, wn_ref, wr_ref, wrs_ref, wk_ref, cc_ref, ss_ref, out_ref):
    c = c_ref[...]
    qn = jnp.dot(c, wn_ref[...], preferred_element_type=F32)
    qlat = jnp.dot(qn.astype(BF16), wk_ref[0], preferred_element_type=F32)
    r1 = jnp.dot(c, wr_ref[...], preferred_element_type=F32)
    r2 = jnp.dot(c, wrs_ref[...], preferred_element_type=F32)
    out_ref[:, :KV_LORA] = qlat.astype(BF16)
    out_ref[:, KV_LORA:] = (r1 * cc_ref[...] + r2 * ss_ref[...]).astype(BF16)


def mla_q(cq, wn, wr, wrs, wk_t, cc, ss):
    n = cq.shape[0]
    head_cols = lambda i, h: (0, h)
    return pl.pallas_call(
        _mla_q_body,
        grid=(n // TM, MLA_HEADS),
        in_specs=[
            pl.BlockSpec((TM, Q_LORA), lambda i, h: (i, 0)),
            pl.BlockSpec((Q_LORA, QK_NOPE), head_cols),
            pl.BlockSpec((Q_LORA, LANES), head_cols),
            pl.BlockSpec((Q_LORA, LANES), head_cols),
            pl.BlockSpec((1, QK_NOPE, KV_LORA), lambda i, h: (h, 0, 0)),
            pl.BlockSpec((TM, LANES), lambda i, h: (i, 0)),
            pl.BlockSpec((TM, LANES), lambda i, h: (i, 0)),
        ],
        out_specs=pl.BlockSpec((TM, QK_WIDE), lambda i, h: (i, h)),
        out_shape=jax.ShapeDtypeStruct((n, MLA_HEADS * QK_WIDE), BF16),
        compiler_params=_cparams(("parallel", "arbitrary")),
        name="mla_q",
    )(cq, wn, wr, wrs, wk_t, cc, ss)


def _flash_body(q_ref, k_ref, o_ref, qs_ref, acc_ref, m_ref, l_ref):
    qi = pl.program_id(1)
    ki = pl.program_id(2)
    rows_all = MLA_HEADS * TQ

    @pl.when(ki == 0)
    def _():
        for h in range(MLA_HEADS):
            qs_ref[h * TQ:(h + 1) * TQ, :] = q_ref[:, h * QK_WIDE:(h + 1) * QK_WIDE]
        m_ref[...] = jnp.full(m_ref.shape, -jnp.inf, F32)
        l_ref[...] = jnp.zeros(l_ref.shape, F32)
        acc_ref[...] = jnp.zeros(acc_ref.shape, F32)

    def update(masked):
        kb = k_ref[...]
        vb = kb[:, :KV_LORA]
        for r0 in range(0, rows_all, FLASH_ROWS):
            rs = slice(r0, r0 + FLASH_ROWS)
            s = lax.dot_general(qs_ref[rs, :], kb, (((1,), (1,)), ((), ())),
                                preferred_element_type=F32) * ATTN_SCALE
            if masked:
                tok = lax.broadcasted_iota(jnp.int32, s.shape, 0) & (TQ - 1)
                key = lax.broadcasted_iota(jnp.int32, s.shape, 1)
                s = jnp.where(key <= tok, s, -jnp.inf)
            m_prev = m_ref[rs, :]
            m_new = jnp.maximum(m_prev, jnp.max(s, axis=-1, keepdims=True))
            p = jnp.exp(s - m_new)
            alpha = jnp.exp(m_prev - m_new)
            l_ref[rs, :] = alpha * l_ref[rs, :] + jnp.sum(p, axis=-1, keepdims=True)
            acc_ref[rs, :] = alpha * acc_ref[rs, :] + jnp.dot(p.astype(BF16), vb, preferred_element_type=F32)
            m_ref[rs, :] = m_new

    @pl.when(ki < qi)
    def _():
        update(False)

    @pl.when(ki == qi)
    def _():
        update(True)
        for h in range(MLA_HEADS):
            hs = slice(h * TQ, (h + 1) * TQ)
            o_ref[:, h * KV_LORA:(h + 1) * KV_LORA] = (acc_ref[hs, :] / l_ref[hs, :]).astype(BF16)


def flash_attention(q_wide, k_wide):
    nq = SEQ // TQ
    rows_all = MLA_HEADS * TQ
    return pl.pallas_call(
        _flash_body,
        grid=(BATCH, nq, nq),
        in_specs=[
            pl.BlockSpec((TQ, MLA_HEADS * QK_WIDE), lambda b, qi, ki: (b * nq + qi, 0)),
            pl.BlockSpec((TQ, QK_WIDE), lambda b, qi, ki: (b * nq + jnp.minimum(ki, qi), 0)),
        ],
        out_specs=pl.BlockSpec((TQ, MLA_HEADS * KV_LORA), lambda b, qi, ki: (b * nq + qi, 0)),
        out_shape=jax.ShapeDtypeStruct((N_PROMPT, MLA_HEADS * KV_LORA), BF16),
        scratch_shapes=[
            pltpu.VMEM((rows_all, QK_WIDE), BF16),
            pltpu.VMEM((rows_all, KV_LORA), F32),
            pltpu.VMEM((rows_all, 1), F32),
            pltpu.VMEM((rows_all, 1), F32),
        ],
        compiler_params=_cparams(("parallel", "parallel", "arbitrary")),
        name="flash_attention",
    )(q_wide, k_wide)


def _paged_body(pt_ref, q_ref, knew_ref, *rest):
    del pt_ref
    npg = PAGES_PER_STEP
    ckv_refs = rest[:npg]
    kr_refs = rest[npg:2 * npg]
    o_ref, acc_ref, m_ref, l_ref = rest[2 * npg:]
    step = pl.program_id(1)
    q = q_ref[0]
    q_lat = q[:, :KV_LORA]
    q_rope = q[:, KV_LORA:KV_LORA + QK_ROPE]
    nt = (((1,), (1,)), ((), ()))

    def online(s, values):
        m_prev = m_ref[...]
        m_new = jnp.maximum(m_prev, jnp.max(s, axis=-1, keepdims=True))
        p = jnp.exp(s - m_new)
        alpha = jnp.exp(m_prev - m_new)
        l_ref[...] = alpha * l_ref[...] + jnp.sum(p, axis=-1, keepdims=True)
        pv = None
        off = 0
        for val in values:
            w = val.shape[0]
            t = jnp.dot(p[:, off:off + w].astype(BF16), val, preferred_element_type=F32)
            pv = t if pv is None else pv + t
            off += w
        acc_ref[...] = alpha * acc_ref[...] + pv
        m_ref[...] = m_new

    @pl.when(step == 0)
    def _():
        m_ref[...] = jnp.full(m_ref.shape, -jnp.inf, F32)
        l_ref[...] = jnp.zeros(l_ref.shape, F32)
        acc_ref[...] = jnp.zeros(acc_ref.shape, F32)
        kn = knew_ref[0]
        s = lax.dot_general(q, kn, nt, preferred_element_type=F32) * ATTN_SCALE
        tok = lax.broadcasted_iota(jnp.int32, s.shape, 0) & (DEC_SEQ - 1)
        key = lax.broadcasted_iota(jnp.int32, s.shape, 1)
        online(jnp.where(key <= tok, s, -jnp.inf), [kn[:, :KV_LORA]])

    vals = [r[0, 0].astype(BF16) for r in ckv_refs]
    parts = [
        lax.dot_general(q_lat, vals[i], nt, preferred_element_type=F32)
        + lax.dot_general(q_rope, kr_refs[i][0, 0].astype(BF16), nt, preferred_element_type=F32)
        for i in range(npg)
    ]
    online(jnp.concatenate(parts, axis=1) * ATTN_SCALE, vals)

    @pl.when(step == pl.num_programs(1) - 1)
    def _():
        o_ref[0] = (acc_ref[...] / l_ref[...]).astype(BF16)


def paged_attention(page_table, q_rows, k_new, ckv_pool, kr_pool, layer):
    npg = PAGES_PER_STEP
    rows = MLA_HEADS * DEC_SEQ

    def page_spec(width, i):
        return pl.BlockSpec((1, 1, PAGE_SIZE, width), lambda b, p, pt: (layer, pt[b, p * npg + i], 0, 0))

    grid_spec = pltpu.PrefetchScalarGridSpec(
        num_scalar_prefetch=1,
        grid=(DEC_BATCH, N_PAGES // npg),
        in_specs=[
            pl.BlockSpec((1, rows, QK_WIDE), lambda b, p, pt: (b, 0, 0)),
            pl.BlockSpec((1, MLSTM_CHUNK_S, QK_WIDE), lambda b, p, pt: (b, 0, 0)),
        ] + [page_spec(KV_LORA, i) for i in range(npg)] + [page_spec(QK_ROPE, i) for i in range(npg)],
        out_specs=pl.BlockSpec((1, rows, KV_LORA), lambda b, p, pt: (b, 0, 0)),
        scratch_shapes=[
            pltpu.VMEM((rows, KV_LORA), F32),
            pltpu.VMEM((rows, 1), F32),
            pltpu.VMEM((rows, 1), F32),
        ],
    )
    return pl.pallas_call(
        _paged_body,
        grid_spec=grid_spec,
        out_shape=jax.ShapeDtypeStruct((DEC_BATCH, rows, KV_LORA), BF16),
        compiler_params=_cparams(("parallel", "arbitrary")),
        name="paged_attention",
    )(page_table, q_rows, k_new, *([ckv_pool] * npg), *([kr_pool] * npg))


def _mla_out_body(ol_ref, wv_ref, wo_ref, r_ref, out_ref, o_scr):
    for h in range(MLA_HEADS):
        oh = jnp.dot(ol_ref[:, h * KV_LORA:(h + 1) * KV_LORA], wv_ref[h], preferred_element_type=F32)
        o_scr[:, h * V_HEAD:(h + 1) * V_HEAD] = oh.astype(BF16)
    out_ref[...] = r_ref[...] + jnp.dot(o_scr[...], wo_ref[...], preferred_element_type=F32)


def mla_out(h, o_lat, wv, wo, *, row_block_offset):
    m = o_lat.shape[0]
    tm = TM_ROW
    return pl.pallas_call(
        _mla_out_body,
        grid=(m // tm,),
        in_specs=[
            pl.BlockSpec((tm, MLA_HEADS * KV_LORA), lambda i: (i, 0)),
            pl.BlockSpec((MLA_HEADS, KV_LORA, V_HEAD), lambda i: (0, 0, 0)),
            pl.BlockSpec((MLA_HEADS * V_HEAD, D_MODEL), lambda i: (0, 0)),
            pl.BlockSpec((tm, D_MODEL), lambda i: (i + row_block_offset, 0)),
        ],
        out_specs=pl.BlockSpec((tm, D_MODEL), lambda i: (i + row_block_offset, 0)),
        out_shape=jax.ShapeDtypeStruct(h.shape, F32),
        scratch_shapes=[pltpu.VMEM((tm, MLA_HEADS * V_HEAD), BF16)],
        input_output_aliases={3: 0},
        compiler_params=_cparams(("parallel",)),
        name="mla_out",
    )(o_lat, wv, wo, h)


def _router_body(h_ref, g_ref, w_ref, b_ref, u_ref, e_ref, gate_ref):
    u = _rms(h_ref[...], g_ref[...])
    u_ref[...] = u
    lg = jnp.dot(u.astype(BF16), w_ref[...], preferred_element_type=F32) + b_ref[...]
    lane = lax.broadcasted_iota(jnp.int32, lg.shape, 1)
    far = jnp.int32(LANES)
    lgm = jnp.where(lane < N_GROUPS, lg, -jnp.inf)
    mg = jnp.max(lgm, axis=-1, keepdims=True)
    pg_sel = 1.0 / jnp.sum(jnp.exp(lgm - mg), axis=-1, keepdims=True)
    g_sel = jnp.min(jnp.where(lgm == mg, lane, far), axis=-1, keepdims=True)
    lo = N_GROUPS + EXPERTS_PER_GROUP * g_sel
    in_group = jnp.logical_and(lane >= lo, lane < lo + EXPERTS_PER_GROUP)
    lem = jnp.where(in_group, lg, -jnp.inf)
    me = jnp.max(lem, axis=-1, keepdims=True)
    ee = jnp.exp(lem - me)
    pe = ee / jnp.sum(ee, axis=-1, keepdims=True)
    pe = jnp.where(in_group, pe, -1.0)
    p1 = jnp.max(pe, axis=-1, keepdims=True)
    i1 = jnp.min(jnp.where(pe == p1, lane, far), axis=-1, keepdims=True)
    pe2 = jnp.where(lane == i1, -1.0, pe)
    p2 = jnp.max(pe2, axis=-1, keepdims=True)
    i2 = jnp.min(jnp.where(pe2 == p2, lane, far), axis=-1, keepdims=True)
    norm = pg_sel / (p1 + p2)
    e_ref[...] = jnp.where(lane == 0, i1 - N_GROUPS, jnp.where(lane == 1, i2 - N_GROUPS, 0))
    gate_ref[...] = jnp.where(lane == 0, p1 * norm, jnp.where(lane == 1, p2 * norm, 0.0))


def router(h, g, w, b):
    n, d = h.shape
    tm = TM_ROW
    row = lambda i: (i, 0)
    fixed = lambda i: (0, 0)
    return pl.pallas_call(
        _router_body,
        grid=(n // tm,),
        in_specs=[
            pl.BlockSpec((tm, d), row),
            pl.BlockSpec((1, d), fixed),
            pl.BlockSpec((d, LANES), fixed),
            pl.BlockSpec((1, LANES), fixed),
        ],
        out_specs=[pl.BlockSpec((tm, d), row), pl.BlockSpec((tm, LANES), row), pl.BlockSpec((tm, LANES), row)],
        out_shape=[
            jax.ShapeDtypeStruct((n, d), F32),
            jax.ShapeDtypeStruct((n, LANES), jnp.int32),
            jax.ShapeDtypeStruct((n, LANES), F32),
        ],
        compiler_params=_cparams(("parallel",)),
        name="router",
    )(h, g, w, b)


def _gather_body(idx_ref, src_ref, out_ref, sem):
    base = pl.program_id(0) * GATHER_ROWS

    def row_copy(r, src_row):
        return pltpu.make_async_copy(src_ref.at[pl.ds(src_row, 1)], out_ref.at[pl.ds(r, 1)], sem)

    def start(r, carry):
        row_copy(r, idx_ref[base + r]).start()
        return carry

    def wait(r, carry):
        row_copy(r, 0).wait()
        return carry

    lax.fori_loop(0, GATHER_ROWS, start, 0)
    lax.fori_loop(0, GATHER_ROWS, wait, 0)


def gather_rows(src, idx):
    n_out = idx.shape[0]
    width = src.shape[1]
    grid_spec = pltpu.PrefetchScalarGridSpec(
        num_scalar_prefetch=1,
        grid=(n_out // GATHER_ROWS,),
        in_specs=[pl.BlockSpec(memory_space=pl.ANY)],
        out_specs=pl.BlockSpec((GATHER_ROWS, width), lambda i, idx: (i, 0)),
        scratch_shapes=[pltpu.SemaphoreType.DMA(())],
    )
    return pl.pallas_call(
        _gather_body,
        grid_spec=grid_spec,
        out_shape=jax.ShapeDtypeStruct((n_out, width), src.dtype),
        compiler_params=_cparams(("arbitrary",)),
        name="gather_rows",
    )(idx, src)


def _experts_body(be_ref, nu_ref, x_ref, w1_ref, w3_ref, w2_ref, y_ref, w1b, w3b, w2b):
    b = pl.program_id(0)
    changed = jnp.logical_or(b == 0, be_ref[b] != be_ref[jnp.maximum(b - 1, 0)])

    @pl.when(changed)
    def _():
        w1b[...] = w1_ref[0, 0].astype(BF16)
        w3b[...] = w3_ref[0, 0].astype(BF16)
        w2b[...] = w2_ref[0, 0].astype(BF16)

    @pl.when(b < nu_ref[0])
    def _():
        x = x_ref[...].astype(BF16)
        a = jnp.dot(x, w1b[...], preferred_element_type=F32)
        g = jnp.dot(x, w3b[...], preferred_element_type=F32)
        mid = (a * _sigmoid(a) * g).astype(BF16)
        y_ref[...] = jnp.dot(mid, w2b[...], preferred_element_type=F32)

    @pl.when(b >= nu_ref[0])
    def _():
        y_ref[...] = jnp.zeros(y_ref.shape, F32)


def experts_ffn(block_expert, n_used, x_pad, w1, w3, w2, layer):
    n_slots, d = x_pad.shape
    n_blocks = n_slots // MOE_BM
    grid_spec = pltpu.PrefetchScalarGridSpec(
        num_scalar_prefetch=2,
        grid=(n_blocks,),
        in_specs=[
            pl.BlockSpec((MOE_BM, d), lambda b, be, nu: (b, 0)),
            pl.BlockSpec((1, 1, d, D_EXPERT), lambda b, be, nu: (layer, be[b], 0, 0)),
            pl.BlockSpec((1, 1, d, D_EXPERT), lambda b, be, nu: (layer, be[b], 0, 0)),
            pl.BlockSpec((1, 1, D_EXPERT, d), lambda b, be, nu: (layer, be[b], 0, 0)),
        ],
        out_specs=pl.BlockSpec((MOE_BM, d), lambda b, be, nu: (b, 0)),
        scratch_shapes=[
            pltpu.VMEM((d, D_EXPERT), BF16),
            pltpu.VMEM((d, D_EXPERT), BF16),
            pltpu.VMEM((D_EXPERT, d), BF16),
        ],
    )
    return pl.pallas_call(
        _experts_body,
        grid_spec=grid_spec,
        out_shape=jax.ShapeDtypeStruct((n_slots, d), F32),
        compiler_params=_cparams(("arbitrary",)),
        name="experts_ffn",
    )(block_expert, n_used, x_pad, w1, w3, w2)


def moe_dispatch(experts):
    a = N_TOK * TOP_K
    n_blocks = a // MOE_BM + N_EXPERTS
    flat_e = experts.reshape(a)
    order = jnp.argsort(flat_e)
    sorted_e = flat_e[order]
    counts = jnp.bincount(flat_e, length=N_EXPERTS)
    padded = (counts + MOE_BM - 1) // MOE_BM * MOE_BM
    start = jnp.cumsum(counts) - counts
    pad_end = jnp.cumsum(padded)
    pad_start = pad_end - padded
    slot_sorted = (pad_start[sorted_e] + jnp.arange(a) - start[sorted_e]).astype(jnp.int32)
    slot = jnp.zeros((a,), jnp.int32).at[order].set(slot_sorted)
    token_of_slot = jnp.zeros((n_blocks * MOE_BM,), jnp.int32).at[slot].set(jnp.arange(a, dtype=jnp.int32) // TOP_K)
    block_expert = jnp.minimum(
        jnp.searchsorted(pad_end, jnp.arange(n_blocks) * MOE_BM, side='right'), N_EXPERTS - 1).astype(jnp.int32)
    n_used = (pad_end[-1:] // MOE_BM).astype(jnp.int32)
    return slot, token_of_slot, block_expert, n_used


def _ple_body(h_ref, y_ref, gate_ref, g_ref, wg_ref, p_ref, wp_ref, gf_ref, out_ref, *, final):
    gt = _round_bf16(gate_ref[...])
    h2 = h_ref[...] + (gt[:, 0:1] * _round_bf16(y_ref[:, :D_MODEL]) + gt[:, 1:2] * _round_bf16(y_ref[:, D_MODEL:]))
    xn = _rms(h2, g_ref[...]).astype(BF16)
    gate = _sigmoid(jnp.dot(xn, wg_ref[...], preferred_element_type=F32))
    emb = jnp.dot(p_ref[...].astype(BF16), wp_ref[...], preferred_element_type=F32)
    h3 = h2 + gate * emb
    out_ref[...] = _rms(h3, gf_ref[...]) if final else h3


def ple(h, y_pairs, gates, g, wg, p, wp, g_final, *, final):
    n, d = h.shape
    tm = TM_ROW
    row = lambda i: (i, 0)
    fixed = lambda i: (0, 0)
    return pl.pallas_call(
        functools.partial(_ple_body, final=final),
        grid=(n // tm,),
        in_specs=[
            pl.BlockSpec((tm, d), row),
            pl.BlockSpec((tm, TOP_K * d), row),
            pl.BlockSpec((tm, LANES), row),
            pl.BlockSpec((1, d), fixed),
            pl.BlockSpec((d, d), fixed),
            pl.BlockSpec((tm, PLE_DIM), row),
            pl.BlockSpec((PLE_DIM, d), fixed),
            pl.BlockSpec((1, d), fixed),
        ],
        out_specs=pl.BlockSpec((tm, d), row),
        out_shape=jax.ShapeDtypeStruct((n, d), F32),
        compiler_params=_cparams(("parallel",)),
        name="ple",
    )(h, y_pairs, gates, g, wg, p, wp, g_final)


def _pad_cols(w, width):
    return jnp.pad(w, ((0, 0), (0, width - w.shape[1])))


def _rope_tables():
    pos = jnp.concatenate([jnp.tile(jnp.arange(SEQ), BATCH), jnp.tile(PAST_LEN + jnp.arange(DEC_SEQ), DEC_BATCH)])
    inv_freq = ROPE_THETA ** (-jnp.arange(0, QK_ROPE, 2, dtype=F32) / QK_ROPE)
    ang = pos.astype(F32)[:, None] * inv_freq[None, :]
    cos, sin = jnp.cos(ang), jnp.sin(ang)
    cc = jnp.concatenate([cos, cos], axis=1)
    ss = jnp.concatenate([-sin, sin], axis=1)
    return jnp.tile(cc, (1, LANES // QK_ROPE)), jnp.tile(ss, (1, LANES // QK_ROPE))


def _swap_halves(w):
    half = w.shape[-1] // 2
    return jnp.concatenate([w[..., half:], w[..., :half]], axis=-1)


def _mlstm_layer(h, j, ln, w_in, b_if, g_head, w_out, state_c, state_n, state_m):
    w_main = w_in[:, :MLSTM_MAIN].astype(BF16)
    w_gate = _pad_cols(w_in[:, MLSTM_MAIN:], LANES).astype(BF16)
    qkv, o_gate, gates = norm_proj(h, ln.reshape(1, D_MODEL), w_main, w_gate)
    bias = _pad_cols(b_if.reshape(1, 2 * MLSTM_HEADS), LANES)
    gh = g_head.reshape(1, D_MODEL)

    zc = jnp.zeros((BATCH, MLSTM_HEADS, MLSTM_DK, MLSTM_DV), F32)
    zn = jnp.zeros((BATCH, MLSTM_HEADS, MLSTM_DK), F32)
    zm = jnp.zeros((BATCH, 1, LANES), F32)
    gated_p, c_p, n_p, m_p = mlstm_scan(qkv, o_gate, gates, bias, gh, zc, zn, zm, n_seq=BATCH,
                                        n_chunks=SEQ // MLSTM_CHUNK_P, chunk=MLSTM_CHUNK_P, valid=MLSTM_CHUNK_P)

    def pad_rows(a):
        a = a[N_PROMPT:].reshape(DEC_BATCH, DEC_SEQ, a.shape[1])
        return jnp.pad(a, ((0, 0), (0, MLSTM_CHUNK_S - DEC_SEQ), (0, 0))).reshape(-1, a.shape[2])

    m0 = _pad_cols(state_m[j], LANES).reshape(DEC_BATCH, 1, LANES)
    gated_s, c_s, n_s, m_s = mlstm_scan(pad_rows(qkv), pad_rows(o_gate), pad_rows(gates), bias, gh, state_c[j],
                                        state_n[j], m0, n_seq=DEC_BATCH, n_chunks=1, chunk=MLSTM_CHUNK_S,
                                        valid=DEC_SEQ)
    gated_s = gated_s.reshape(DEC_BATCH, MLSTM_CHUNK_S, D_MODEL)[:, :DEC_SEQ].reshape(N_SAMPLE, D_MODEL)

    w_o = w_out.astype(BF16)
    h = linear_res(h, gated_p, w_o, row_block_offset=0)
    h = linear_res(h, gated_s, w_o, row_block_offset=N_PROMPT // TM)
    state_p = (c_p, n_p, m_p[:, 0, :MLSTM_HEADS])
    state_s = (c_s, n_s, m_s[:, 0, :MLSTM_HEADS])
    return h, state_p, state_s


def _mla_layer(h, j, ln, w_a, g_q, w_qb, g_kv, w_kvb, w_o, cc, ss, cache_ckv, cache_kr, page_table):
    w_kr = w_a[:, Q_LORA + KV_LORA:]
    w_a_ext = jnp.concatenate([w_a, _swap_halves(w_kr)], axis=1).astype(BF16)
    cq, ckv, kr, k_wide = mla_a(h, ln.reshape(1, D_MODEL), w_a_ext, g_q.reshape(1, Q_LORA),
                                g_kv.reshape(1, KV_LORA), cc, ss)

    wq = w_qb.reshape(Q_LORA, MLA_HEADS, QK_NOPE + QK_ROPE)
    w_nope = wq[:, :, :QK_NOPE].reshape(Q_LORA, MLA_HEADS * QK_NOPE).astype(BF16)
    w_rope = wq[:, :, QK_NOPE:]
    pad = ((0, 0), (0, 0), (0, LANES - QK_ROPE))
    w_r = jnp.pad(w_rope, pad).reshape(Q_LORA, MLA_HEADS * LANES).astype(BF16)
    w_rs = jnp.pad(_swap_halves(w_rope), pad).reshape(Q_LORA, MLA_HEADS * LANES).astype(BF16)
    wkv = w_kvb.reshape(KV_LORA, MLA_HEADS, QK_NOPE + V_HEAD)
    wk_t = jnp.transpose(wkv[:, :, :QK_NOPE], (1, 2, 0)).astype(BF16)
    wv = jnp.transpose(wkv[:, :, QK_NOPE:], (1, 0, 2)).astype(BF16)
    q_wide = mla_q(cq, w_nope, w_r, w_rs, wk_t, cc, ss)

    o_lat_p = flash_attention(q_wide, k_wide)

    q_s = q_wide[N_PROMPT:].reshape(DEC_BATCH, DEC_SEQ, MLA_HEADS, QK_WIDE)
    q_s = jnp.transpose(q_s, (0, 2, 1, 3)).reshape(DEC_BATCH, MLA_HEADS * DEC_SEQ, QK_WIDE)
    k_new = jnp.pad(k_wide[N_PROMPT:].reshape(DEC_BATCH, DEC_SEQ, QK_WIDE),
                    ((0, 0), (0, MLSTM_CHUNK_S - DEC_SEQ), (0, 0)))
    o_s = paged_attention(page_table, q_s, k_new, cache_ckv, cache_kr, j)
    o_lat_s = jnp.transpose(o_s.reshape(DEC_BATCH, MLA_HEADS, DEC_SEQ, KV_LORA), (0, 2, 1, 3))
    o_lat_s = o_lat_s.reshape(N_SAMPLE, MLA_HEADS * KV_LORA)

    wo = w_o.astype(BF16)
    h = mla_out(h, o_lat_p, wv, wo, row_block_offset=0)
    h = mla_out(h, o_lat_s, wv, wo, row_block_offset=N_PROMPT // TM_ROW)
    return h, ckv, kr


def _moe_ple_layer(h, i, ln_ffn, w_rg, b_rg, w_re, b_re, w1, w3, w2, p_all, ple_g, ple_wg, ple_wp, ln_final):
    w_r = _pad_cols(jnp.concatenate([w_rg, w_re], axis=1), LANES).astype(BF16)
    b_r = _pad_cols(jnp.concatenate([b_rg, b_re]).reshape(1, -1), LANES)
    u, e_idx, gates = router(h, ln_ffn.reshape(1, D_MODEL), w_r, b_r)
    slot, token_of_slot, block_expert, n_used = moe_dispatch(e_idx[:, :TOP_K])
    x_pad = gather_rows(u, token_of_slot)
    y_pad = experts_ffn(block_expert, n_used, x_pad, w1, w3, w2, i)
    y_pairs = gather_rows(y_pad, slot).reshape(N_TOK, TOP_K * D_MODEL)
    return ple(h, y_pairs, gates, ple_g.reshape(1, D_MODEL), ple_wg.astype(BF16), p_all, ple_wp.astype(BF16),
               ln_final.reshape(1, D_MODEL), final=(i == DEPTH - 1))


def kernel(x_prompt, x_sample, cache_mla_ckv, cache_mla_krope, state_mlstm_C, state_mlstm_n, state_mlstm_m,
           page_table, p_prompt, p_sample, ln_mix, ln_ffn, ln_final, mlstm_w_in, mlstm_b_if, mlstm_g_head,
           mlstm_w_out, mla_w_a, mla_g_q, mla_w_qb, mla_g_kv, mla_w_kvb, mla_w_o, moe_w_rg, moe_b_rg, moe_w_re,
           moe_b_re, moe_w1, moe_w3, moe_w2, ple_g, ple_w_gate, ple_w_proj):
    h = jnp.concatenate([x_prompt.reshape(N_PROMPT, D_MODEL), x_sample.reshape(N_SAMPLE, D_MODEL)], axis=0)
    cc, ss = _rope_tables()
    ckv_rows, kr_rows, states_p, states_s = [], [], [], []
    for i in range(DEPTH):
        j = i // 2
        if i % 2 == 0:
            h, st_p, st_s = _mlstm_layer(h, j, ln_mix[i], mlstm_w_in[j], mlstm_b_if[j], mlstm_g_head[j],
                                         mlstm_w_out[j], state_mlstm_C, state_mlstm_n, state_mlstm_m)
            states_p.append(st_p)
            states_s.append(st_s)
        else:
            h, ckv, kr = _mla_layer(h, j, ln_mix[i], mla_w_a[j], mla_g_q[j], mla_w_qb[j], mla_g_kv[j],
                                    mla_w_kvb[j], mla_w_o[j], cc, ss, cache_mla_ckv, cache_mla_krope, page_table)
            ckv_rows.append(ckv)
            kr_rows.append(kr)
        p_all = jnp.concatenate([p_prompt[i].reshape(N_PROMPT, PLE_DIM), p_sample[i].reshape(N_SAMPLE, PLE_DIM)])
        h = _moe_ple_layer(h, i, ln_ffn[i], moe_w_rg[i], moe_b_rg[i], moe_w_re[i], moe_b_re[i], moe_w1, moe_w3,
                           moe_w2, p_all, ple_g[i], ple_w_gate[i], ple_w_proj[i], ln_final)

    ckv_all = jnp.stack(ckv_rows)
    kr_all = jnp.stack(kr_rows)

    def split(a, width):
        return (a[:, :N_PROMPT].reshape(-1, BATCH, SEQ, width), a[:, N_PROMPT:].reshape(-1, DEC_BATCH, DEC_SEQ, width))

    ckv_p, ckv_s = split(ckv_all, KV_LORA)
    kr_p, kr_s = split(kr_all, QK_ROPE)
    c_p, n_p, m_p = (jnp.stack(t) for t in zip(*states_p))
    c_s, n_s, m_s = (jnp.stack(t) for t in zip(*states_s))
    y_prompt = h[:N_PROMPT].reshape(BATCH, SEQ, D_MODEL)
    y_sample = h[N_PROMPT:].reshape(DEC_BATCH, DEC_SEQ, D_MODEL)
    return (y_prompt, y_sample, ckv_p, kr_p, c_p, n_p, m_p, ckv_s, kr_s, c_s, n_s, m_s)
```

```python
import functools

import jax
import jax.numpy as jnp
from jax import lax
from jax.experimental import pallas as pl
from jax.experimental.pallas import tpu as pltpu

F32 = jnp.float32
BF16 = jnp.bfloat16
HIGHEST = lax.Precision.HIGHEST

D_MODEL = 2048
BATCH = 2
SEQ = 4096
DEPTH = 4
DEC_BATCH = 128
DEC_SEQ = 4
PAST_LEN = 8192
PAGE_SIZE = 128
N_PAGES = PAST_LEN // PAGE_SIZE
EPS = 1e-6

MLSTM_HEADS = 8
MLSTM_DV = D_MODEL // MLSTM_HEADS
MLSTM_DK = MLSTM_DV // 2
MLSTM_QK = MLSTM_HEADS * MLSTM_DK
GATE_SOFTCAP = 15.0
MLSTM_MAIN = 2 * MLSTM_QK + 2 * D_MODEL
QKV_TILES = (2 * MLSTM_QK + D_MODEL) // MLSTM_QK

MLA_HEADS = 16
QK_NOPE = 128
QK_ROPE = 64
V_HEAD = 128
Q_LORA = 512
KV_LORA = 512
ROPE_THETA = 10000.0
ATTN_SCALE = (QK_NOPE + QK_ROPE) ** -0.5
QK_WIDE = KV_LORA + 128

N_GROUPS = 8
EXPERTS_PER_GROUP = 8
N_EXPERTS = N_GROUPS * EXPERTS_PER_GROUP
TOP_K = 2
D_EXPERT = 512
PLE_DIM = 256

N_PROMPT = BATCH * SEQ
N_SAMPLE = DEC_BATCH * DEC_SEQ
N_TOK = N_PROMPT + N_SAMPLE

LANES = 128
NEG = -1e30
VMEM_LIMIT = 56 * 1024 * 1024

TM = 512
TM_ROW = 256
MLSTM_CHUNK_P = 64
MLSTM_CHUNK_S = 16
TQ = 256
TK = 1024
Q_PER_K = TK // TQ
FLASH_ROWS = 512
EXP2_SCALE = ATTN_SCALE * 1.4426950408889634
PAGES_PER_STEP = 16
Q_HEADS_PER_STEP = 4
MOE_BM = 256
GATHER_ROWS = 256


def _cparams(sem):
    return pltpu.CompilerParams(dimension_semantics=sem, vmem_limit_bytes=VMEM_LIMIT)


def _rms(x, g):
    return x * lax.rsqrt(jnp.mean(x * x, axis=-1, keepdims=True) + EPS) * g


def _sigmoid(x):
    return 1.0 / (1.0 + jnp.exp(-x))


def _round_bf16(x):
    return x.astype(BF16).astype(F32)


def _norm_proj_body(h_ref, g_ref, w_ref, wg_ref, qkv_ref, o_ref, gate_ref, xn_ref):
    j = pl.program_id(1)

    @pl.when(j == 0)
    def _():
        xn = _rms(h_ref[...], g_ref[...]).astype(BF16)
        xn_ref[...] = xn
        gate_ref[...] = jnp.dot(xn, wg_ref[...], preferred_element_type=F32)

    acc = jnp.dot(xn_ref[...], w_ref[...], preferred_element_type=F32)

    @pl.when(j == 0)
    def _():
        qkv_ref[...] = (acc * MLSTM_DK ** -0.5).astype(BF16)

    @pl.when(jnp.logical_and(j > 0, j < QKV_TILES))
    def _():
        qkv_ref[...] = acc.astype(BF16)

    @pl.when(j >= QKV_TILES)
    def _():
        o_ref[...] = acc


def norm_proj(h, g, w, wg):
    n, d = h.shape
    tn = MLSTM_QK
    return pl.pallas_call(
        _norm_proj_body,
        grid=(n // TM, MLSTM_MAIN // tn),
        in_specs=[
            pl.BlockSpec((TM, d), lambda i, j: (i, 0)),
            pl.BlockSpec((1, d), lambda i, j: (0, 0)),
            pl.BlockSpec((d, tn), lambda i, j: (0, j)),
            pl.BlockSpec((d, LANES), lambda i, j: (0, 0)),
        ],
        out_specs=[
            pl.BlockSpec((TM, tn), lambda i, j: (i, jnp.minimum(j, QKV_TILES - 1))),
            pl.BlockSpec((TM, tn), lambda i, j: (i, jnp.maximum(j - QKV_TILES, 0))),
            pl.BlockSpec((TM, LANES), lambda i, j: (i, 0)),
        ],
        out_shape=[
            jax.ShapeDtypeStruct((n, 2 * MLSTM_QK + D_MODEL), BF16),
            jax.ShapeDtypeStruct((n, D_MODEL), F32),
            jax.ShapeDtypeStruct((n, LANES), F32),
        ],
        scratch_shapes=[pltpu.VMEM((TM, d), BF16)],
        compiler_params=_cparams(("parallel", "arbitrary")),
        name="norm_proj",
    )(h, g, w, wg)


def _linear_res_body(x_ref, w_ref, r_ref, o_ref):
    o_ref[...] = r_ref[...] + jnp.dot(x_ref[...], w_ref[...], preferred_element_type=F32)


def linear_res(h, x, w, *, row_block_offset, tn=1024):
    m, k = x.shape
    d = w.shape[1]
    return pl.pallas_call(
        _linear_res_body,
        grid=(m // TM, d // tn),
        in_specs=[
            pl.BlockSpec((TM, k), lambda i, j: (i, 0)),
            pl.BlockSpec((k, tn), lambda i, j: (0, j)),
            pl.BlockSpec((TM, tn), lambda i, j: (i + row_block_offset, j)),
        ],
        out_specs=pl.BlockSpec((TM, tn), lambda i, j: (i + row_block_offset, j)),
        out_shape=jax.ShapeDtypeStruct(h.shape, F32),
        input_output_aliases={2: 0},
        compiler_params=_cparams(("parallel", "parallel")),
        name="linear_res",
    )(x, w, h)


def _mlstm_body(q_ref, k_ref, v_ref, o_ref, gt_ref, bif_ref, gh_ref, c0_ref, n0_ref, m0_ref,
                out_ref, c_ref, n_ref, m_ref, *, chunk, valid):
    L = chunk

    @pl.when(pl.program_id(1) == 0)
    def _():
        c_ref[...] = c0_ref[...]
        n_ref[...] = n0_ref[...]
        m_ref[...] = m0_ref[...]

    gc = GATE_SOFTCAP * jnp.tanh((gt_ref[...] + bif_ref[...]) / GATE_SOFTCAP)
    logf_all = jnp.minimum(gc, 0.0) - jnp.log1p(jnp.exp(-jnp.abs(gc)))
    logi_all = gc
    if valid < L:
        live = lax.broadcasted_iota(jnp.int32, (L, 1), 0) < valid
        logi_all = jnp.where(live, gc, NEG)
        logf_all = jnp.where(live, logf_all, 0.0)
    rows = lax.broadcasted_iota(jnp.int32, (L, L), 0)
    cols = lax.broadcasted_iota(jnp.int32, (L, L), 1)
    causal = rows >= cols
    eye = rows == cols
    b_all = jnp.dot(causal.astype(F32), logf_all, precision=HIGHEST, preferred_element_type=F32)

    def as_row(col):
        return jnp.sum(jnp.where(eye, col, 0.0), axis=0, keepdims=True)

    for hd in range(MLSTM_HEADS):
        b_col = b_all[:, MLSTM_HEADS + hd:MLSTM_HEADS + hd + 1]
        li_col = logi_all[:, hd:hd + 1]
        b_row = as_row(b_col)
        li_row = as_row(li_col)
        m0 = m_ref[0, :, hd:hd + 1]
        d = jnp.where(causal, b_col - b_row + li_row, NEG)
        carry = b_col + m0
        m_row = jnp.maximum(carry, jnp.max(d, axis=-1, keepdims=True))
        q = q_ref[:, hd * MLSTM_DK:(hd + 1) * MLSTM_DK]
        k = k_ref[:, hd * MLSTM_DK:(hd + 1) * MLSTM_DK]
        v = v_ref[:, hd * MLSTM_DV:(hd + 1) * MLSTM_DV]
        qk = lax.dot_general(q, k, (((1,), (1,)), ((), ())), preferred_element_type=F32)
        s = qk * jnp.exp(d - m_row)
        w_prev = jnp.exp(carry - m_row)
        c_old = c_ref[0, hd]
        n_old = n_ref[0, hd:hd + 1, :]
        num = (jnp.dot(s.astype(BF16), v, preferred_element_type=F32)
               + w_prev * jnp.dot(q, c_old.astype(BF16), preferred_element_type=F32))
        den = (jnp.sum(s, axis=-1, keepdims=True)
               + w_prev * jnp.sum(q.astype(F32) * _round_bf16(n_old), axis=-1, keepdims=True))
        hh = num / jnp.maximum(jnp.abs(den), jnp.exp(-m_row))

        b_end = b_col[L - 1:L, :]
        dend = b_end - b_col + li_col
        m_new = jnp.maximum(b_end + m0, jnp.max(dend, axis=0, keepdims=True))
        a_col = jnp.exp(dend - m_new)
        keep = jnp.exp(b_end + m0 - m_new)
        kf = k.astype(F32)
        upd = lax.dot_general((a_col * kf).astype(BF16), v, (((0,), (0,)), ((), ())), preferred_element_type=F32)
        c_ref[0, hd] = keep * c_old + upd
        n_ref[0, hd:hd + 1, :] = keep * n_old + jnp.sum(_round_bf16(a_col) * kf, axis=0, keepdims=True)
        m_ref[0, :, hd:hd + 1] = m_new

        sl = slice(hd * MLSTM_DV, (hd + 1) * MLSTM_DV)
        hn = hh * lax.rsqrt(jnp.mean(hh * hh, axis=-1, keepdims=True) + EPS) * gh_ref[:, sl]
        out_ref[:, sl] = (_sigmoid(o_ref[:, sl]) * hn).astype(BF16)


def mlstm_scan(qkv, o_gate, gates, b_if, g_head, c0, n0, m0, *, n_seq, n_chunks, chunk, valid):
    rows = n_seq * n_chunks * chunk
    row = lambda s, c: s * n_chunks + c
    body = functools.partial(_mlstm_body, chunk=chunk, valid=valid)
    return pl.pallas_call(
        body,
        grid=(n_seq, n_chunks),
        in_specs=[
            pl.BlockSpec((chunk, MLSTM_QK), lambda s, c: (row(s, c), 0)),
            pl.BlockSpec((chunk, MLSTM_QK), lambda s, c: (row(s, c), 1)),
            pl.BlockSpec((chunk, D_MODEL), lambda s, c: (row(s, c), 1)),
            pl.BlockSpec((chunk, D_MODEL), lambda s, c: (row(s, c), 0)),
            pl.BlockSpec((chunk, LANES), lambda s, c: (row(s, c), 0)),
            pl.BlockSpec((1, LANES), lambda s, c: (0, 0)),
            pl.BlockSpec((1, D_MODEL), lambda s, c: (0, 0)),
            pl.BlockSpec((1, MLSTM_HEADS, MLSTM_DK, MLSTM_DV), lambda s, c: (s, 0, 0, 0)),
            pl.BlockSpec((1, MLSTM_HEADS, MLSTM_DK), lambda s, c: (s, 0, 0)),
            pl.BlockSpec((1, 1, LANES), lambda s, c: (s, 0, 0)),
        ],
        out_specs=[
            pl.BlockSpec((chunk, D_MODEL), lambda s, c: (row(s, c), 0)),
            pl.BlockSpec((1, MLSTM_HEADS, MLSTM_DK, MLSTM_DV), lambda s, c: (s, 0, 0, 0)),
            pl.BlockSpec((1, MLSTM_HEADS, MLSTM_DK), lambda s, c: (s, 0, 0)),
            pl.BlockSpec((1, 1, LANES), lambda s, c: (s, 0, 0)),
        ],
        out_shape=[
            jax.ShapeDtypeStruct((rows, D_MODEL), BF16),
            jax.ShapeDtypeStruct((n_seq, MLSTM_HEADS, MLSTM_DK, MLSTM_DV), F32),
            jax.ShapeDtypeStruct((n_seq, MLSTM_HEADS, MLSTM_DK), F32),
            jax.ShapeDtypeStruct((n_seq, 1, LANES), F32),
        ],
        compiler_params=_cparams(("parallel", "arbitrary")),
        name="mlstm_scan",
    )(qkv, qkv, qkv, o_gate, gates, b_if, g_head, c0, n0, m0)


def _mla_a_body(h_ref, g_ref, w_ref, gq_ref, gkv_ref, cc_ref, ss_ref, cq_ref, ckv_ref, kr_ref, kw_ref):
    xn = _rms(h_ref[...], g_ref[...]).astype(BF16)
    a = jnp.dot(xn, w_ref[...], preferred_element_type=F32)
    cq_ref[...] = _rms(a[:, :Q_LORA], gq_ref[...]).astype(BF16)
    ckv = _rms(a[:, Q_LORA:Q_LORA + KV_LORA], gkv_ref[...])
    ckv_ref[...] = ckv
    t = a[:, Q_LORA + KV_LORA:]
    kr = t[:, :QK_ROPE] * cc_ref[:, :QK_ROPE] + t[:, QK_ROPE:] * ss_ref[:, :QK_ROPE]
    kr_ref[...] = kr
    kw_ref[:, :KV_LORA] = ckv.astype(BF16)
    kw_ref[:, KV_LORA:] = jnp.concatenate([kr, jnp.zeros_like(kr)], axis=1).astype(BF16)


def mla_a(h, g, w, gq, gkv, cc, ss):
    n, d = h.shape
    wa = w.shape[1]
    tm = TM_ROW
    row = lambda i: (i, 0)
    fixed = lambda i: (0, 0)
    return pl.pallas_call(
        _mla_a_body,
        grid=(n // tm,),
        in_specs=[
            pl.BlockSpec((tm, d), row),
            pl.BlockSpec((1, d), fixed),
            pl.BlockSpec((d, wa), fixed),
            pl.BlockSpec((1, Q_LORA), fixed),
            pl.BlockSpec((1, KV_LORA), fixed),
            pl.BlockSpec((tm, LANES), row),
            pl.BlockSpec((tm, LANES), row),
        ],
        out_specs=[
            pl.BlockSpec((tm, Q_LORA), row),
            pl.BlockSpec((tm, KV_LORA), row),
            pl.BlockSpec((tm, QK_ROPE), row),
            pl.BlockSpec((tm, QK_WIDE), row),
        ],
        out_shape=[
            jax.ShapeDtypeStruct((n, Q_LORA), BF16),
            jax.ShapeDtypeStruct((n, KV_LORA), F32),
            jax.ShapeDtypeStruct((n, QK_ROPE), F32),
            jax.ShapeDtypeStruct((n, QK_WIDE), BF16),
        ],
        compiler_params=_cparams(("parallel",)),
        name="mla_a",
    )(h, g, w, gq, gkv, cc, ss)


def _mla_q_body(c_ref, wn_ref, wr_ref, wrs_ref, wk_ref, cc_ref, ss_ref, out_ref):
    c = c_ref[...]
    qn = jnp.dot(c, wn_ref[...], preferred_element_type=F32).astype(BF16)
    r1 = jnp.dot(c, wr_ref[...], preferred_element_type=F32)
    r2 = jnp.dot(c, wrs_ref[...], preferred_element_type=F32)
    cc = cc_ref[...]
    ss = ss_ref[...]
    for h in range(Q_HEADS_PER_STEP):
        hs = slice(h * LANES, (h + 1) * LANES)
        qlat = jnp.dot(qn[:, hs], wk_ref[h], preferred_element_type=F32)
        out_ref[:, h * QK_WIDE:h * QK_WIDE + KV_LORA] = qlat.astype(BF16)
        out_ref[:, h * QK_WIDE + KV_LORA:(h + 1) * QK_WIDE] = (r1[:, hs] * cc + r2[:, hs] * ss).astype(BF16)


def mla_q(cq, wn, wr, wrs, wk_t, cc, ss):
    n = cq.shape[0]
    hp = Q_HEADS_PER_STEP
    head_cols = lambda i, h: (0, h)
    return pl.pallas_call(
        _mla_q_body,
        grid=(n // TM, MLA_HEADS // hp),
        in_specs=[
            pl.BlockSpec((TM, Q_LORA), lambda i, h: (i, 0)),
            pl.BlockSpec((Q_LORA, hp * QK_NOPE), head_cols),
            pl.BlockSpec((Q_LORA, hp * LANES), head_cols),
            pl.BlockSpec((Q_LORA, hp * LANES), head_cols),
            pl.BlockSpec((hp, QK_NOPE, KV_LORA), lambda i, h: (h, 0, 0)),
            pl.BlockSpec((TM, LANES), lambda i, h: (i, 0)),
            pl.BlockSpec((TM, LANES), lambda i, h: (i, 0)),
        ],
        out_specs=pl.BlockSpec((TM, hp * QK_WIDE), lambda i, h: (i, h)),
        out_shape=jax.ShapeDtypeStruct((n, MLA_HEADS * QK_WIDE), BF16),
        compiler_params=_cparams(("parallel", "arbitrary")),
        name="mla_q",
    )(cq, wn, wr, wrs, wk_t, cc, ss)


def _flash_body(q_ref, k_ref, o_ref, qs_ref, acc_ref, m_ref, l_ref):
    qi = pl.program_id(1)
    ki = pl.program_id(2)
    rows_all = MLA_HEADS * TQ
    kd = qi // Q_PER_K
    n_tiles = TK // LANES
    n_acc = KV_LORA // LANES

    @pl.when(ki == 0)
    def _():
        for h in range(MLA_HEADS):
            qs_ref[h * TQ:(h + 1) * TQ, :] = q_ref[:, h * QK_WIDE:(h + 1) * QK_WIDE]
        m_ref[...] = jnp.full(m_ref.shape, -jnp.inf, F32)
        l_ref[...] = jnp.zeros(l_ref.shape, F32)
        acc_ref[...] = jnp.zeros(acc_ref.shape, F32)

    def update(masked):
        kb = k_ref[...]
        vb = kb[:, :KV_LORA]
        for r0 in range(0, rows_all, FLASH_ROWS):
            rs = slice(r0, r0 + FLASH_ROWS)
            s = lax.dot_general(qs_ref[rs, :], kb, (((1,), (1,)), ((), ())), preferred_element_type=F32)
            if masked:
                tok = (lax.broadcasted_iota(jnp.int32, s.shape, 0) & (TQ - 1)) + (qi - kd * Q_PER_K) * TQ
                key = lax.broadcasted_iota(jnp.int32, s.shape, 1)
                s = jnp.where(key <= tok, s, -jnp.inf)
            tiles = [s[:, t * LANES:(t + 1) * LANES] for t in range(n_tiles)]
            tile_max = functools.reduce(jnp.maximum, tiles)
            m_prev = m_ref[rs, :]
            m_new = jnp.maximum(m_prev, jnp.max(tile_max, axis=-1, keepdims=True))
            ps = [jnp.exp2((t - m_new) * EXP2_SCALE) for t in tiles]
            alpha = jnp.exp2((m_prev - m_new) * EXP2_SCALE)
            row_sum = jnp.sum(functools.reduce(jnp.add, ps), axis=-1, keepdims=True)
            l_ref[rs, :] = alpha * l_ref[rs, :] + row_sum
            p = jnp.concatenate([t.astype(BF16) for t in ps], axis=1)
            pv = jnp.dot(p, vb, preferred_element_type=F32)
            for c in range(n_acc):
                cs = slice(c * LANES, (c + 1) * LANES)
                acc_ref[rs, cs] = alpha * acc_ref[rs, cs] + pv[:, cs]
            m_ref[rs, :] = m_new

    @pl.when(ki < kd)
    def _():
        update(False)

    @pl.when(ki == kd)
    def _():
        update(True)
        for h in range(MLA_HEADS):
            hs = slice(h * TQ, (h + 1) * TQ)
            inv = 1.0 / l_ref[hs, :]
            for c in range(n_acc):
                cs = slice(c * LANES, (c + 1) * LANES)
                o_ref[:, h * KV_LORA + c * LANES:h * KV_LORA + (c + 1) * LANES] = (acc_ref[hs, cs] * inv).astype(BF16)


def flash_attention(q_wide, k_wide):
    nq = SEQ // TQ
    nk = SEQ // TK
    rows_all = MLA_HEADS * TQ
    return pl.pallas_call(
        _flash_body,
        grid=(BATCH, nq, nk),
        in_specs=[
            pl.BlockSpec((TQ, MLA_HEADS * QK_WIDE), lambda b, qi, ki: (b * nq + qi, 0)),
            pl.BlockSpec((TK, QK_WIDE), lambda b, qi, ki: (b * nk + jnp.minimum(ki, qi // Q_PER_K), 0)),
        ],
        out_specs=pl.BlockSpec((TQ, MLA_HEADS * KV_LORA), lambda b, qi, ki: (b * nq + qi, 0)),
        out_shape=jax.ShapeDtypeStruct((N_PROMPT, MLA_HEADS * KV_LORA), BF16),
        scratch_shapes=[
            pltpu.VMEM((rows_all, QK_WIDE), BF16),
            pltpu.VMEM((rows_all, KV_LORA), F32),
            pltpu.VMEM((rows_all, LANES), F32),
            pltpu.VMEM((rows_all, LANES), F32),
        ],
        compiler_params=_cparams(("parallel", "parallel", "arbitrary")),
        name="flash_attention",
    )(q_wide, k_wide)


def _paged_body(pt_ref, q_ref, knew_ref, *rest):
    del pt_ref
    npg = PAGES_PER_STEP
    ckv_refs = rest[:npg]
    kr_refs = rest[npg:2 * npg]
    o_ref, acc_ref, m_ref, l_ref = rest[2 * npg:]
    step = pl.program_id(1)
    q = q_ref[0]
    q_lat = q[:, :KV_LORA]
    q_rope = q[:, KV_LORA:KV_LORA + QK_ROPE]
    nt = (((1,), (1,)), ((), ()))

    def online(parts, values):
        w = parts[0].shape[1]
        m_prev = m_ref[...]
        m_new = jnp.maximum(m_prev, jnp.max(functools.reduce(jnp.maximum, parts), axis=-1, keepdims=True))
        ps = [jnp.exp2((t - m_new[:, :w]) * EXP2_SCALE) for t in parts]
        alpha = jnp.exp2((m_prev - m_new) * EXP2_SCALE)
        l_ref[...] = alpha * l_ref[...] + jnp.sum(functools.reduce(jnp.add, ps), axis=-1, keepdims=True)
        pv = functools.reduce(jnp.add, [jnp.dot(t.astype(BF16), val, preferred_element_type=F32)
                                        for t, val in zip(ps, values)])
        for c in range(KV_LORA // LANES):
            cs = slice(c * LANES, (c + 1) * LANES)
            acc_ref[:, cs] = alpha * acc_ref[:, cs] + pv[:, cs]
        m_ref[...] = m_new

    @pl.when(step == 0)
    def _():
        m_ref[...] = jnp.full(m_ref.shape, -jnp.inf, F32)
        l_ref[...] = jnp.zeros(l_ref.shape, F32)
        acc_ref[...] = jnp.zeros(acc_ref.shape, F32)
        kn = knew_ref[0]
        s = lax.dot_general(q, kn, nt, preferred_element_type=F32)
        tok = lax.broadcasted_iota(jnp.int32, s.shape, 0) & (DEC_SEQ - 1)
        key = lax.broadcasted_iota(jnp.int32, s.shape, 1)
        online([jnp.where(key <= tok, s, -jnp.inf)], [kn[:, :KV_LORA]])

    vals = [r[0, 0].astype(BF16) for r in ckv_refs]
    parts = [
        lax.dot_general(q_lat, vals[i], nt, preferred_element_type=F32)
        + jnp.dot(q_rope, kr_refs[i][0, 0].astype(BF16), preferred_element_type=F32)
        for i in range(npg)
    ]
    online(parts, vals)

    @pl.when(step == pl.num_programs(1) - 1)
    def _():
        inv = 1.0 / l_ref[...]
        o_ref[0] = (acc_ref[...] * jnp.concatenate([inv] * (KV_LORA // LANES), axis=1)).astype(BF16)


def paged_attention(page_table, q_rows, k_new, ckv_pool, kr_pool_t, layer):
    npg = PAGES_PER_STEP
    rows = MLA_HEADS * DEC_SEQ

    def page_spec(shape, i):
        return pl.BlockSpec((1, 1) + shape, lambda b, p, pt: (layer, pt[b, p * npg + i], 0, 0))

    grid_spec = pltpu.PrefetchScalarGridSpec(
        num_scalar_prefetch=1,
        grid=(DEC_BATCH, N_PAGES // npg),
        in_specs=[
            pl.BlockSpec((1, rows, QK_WIDE), lambda b, p, pt: (b, 0, 0)),
            pl.BlockSpec((1, MLSTM_CHUNK_S, QK_WIDE), lambda b, p, pt: (b, 0, 0)),
        ] + [page_spec((PAGE_SIZE, KV_LORA), i) for i in range(npg)]
        + [page_spec((QK_ROPE, PAGE_SIZE), i) for i in range(npg)],
        out_specs=pl.BlockSpec((1, rows, KV_LORA), lambda b, p, pt: (b, 0, 0)),
        scratch_shapes=[
            pltpu.VMEM((rows, KV_LORA), F32),
            pltpu.VMEM((rows, LANES), F32),
            pltpu.VMEM((rows, LANES), F32),
        ],
    )
    return pl.pallas_call(
        _paged_body,
        grid_spec=grid_spec,
        out_shape=jax.ShapeDtypeStruct((DEC_BATCH, rows, KV_LORA), BF16),
        compiler_params=_cparams(("parallel", "arbitrary")),
        name="paged_attention",
    )(page_table, q_rows, k_new, *([ckv_pool] * npg), *([kr_pool_t] * npg))


def _mla_out_body(ol_ref, wv_ref, wo_ref, r_ref, out_ref, o_scr):
    for h in range(MLA_HEADS):
        oh = jnp.dot(ol_ref[:, h * KV_LORA:(h + 1) * KV_LORA], wv_ref[h], preferred_element_type=F32)
        o_scr[:, h * V_HEAD:(h + 1) * V_HEAD] = oh.astype(BF16)
    out_ref[...] = r_ref[...] + jnp.dot(o_scr[...], wo_ref[...], preferred_element_type=F32)


def mla_out(h, o_lat, wv, wo, *, row_block_offset):
    m = o_lat.shape[0]
    tm = TM_ROW
    return pl.pallas_call(
        _mla_out_body,
        grid=(m // tm,),
        in_specs=[
            pl.BlockSpec((tm, MLA_HEADS * KV_LORA), lambda i: (i, 0)),
            pl.BlockSpec((MLA_HEADS, KV_LORA, V_HEAD), lambda i: (0, 0, 0)),
            pl.BlockSpec((MLA_HEADS * V_HEAD, D_MODEL), lambda i: (0, 0)),
            pl.BlockSpec((tm, D_MODEL), lambda i: (i + row_block_offset, 0)),
        ],
        out_specs=pl.BlockSpec((tm, D_MODEL), lambda i: (i + row_block_offset, 0)),
        out_shape=jax.ShapeDtypeStruct(h.shape, F32),
        scratch_shapes=[pltpu.VMEM((tm, MLA_HEADS * V_HEAD), BF16)],
        input_output_aliases={3: 0},
        compiler_params=_cparams(("parallel",)),
        name="mla_out",
    )(o_lat, wv, wo, h)


def _router_body(h_ref, g_ref, w_ref, b_ref, u_ref, e_ref, gate_ref):
    u = _rms(h_ref[...], g_ref[...])
    u_ref[...] = u
    lg = jnp.dot(u.astype(BF16), w_ref[...], preferred_element_type=F32) + b_ref[...]
    lane = lax.broadcasted_iota(jnp.int32, lg.shape, 1)
    far = jnp.int32(LANES)
    lgm = jnp.where(lane < N_GROUPS, lg, -jnp.inf)
    mg = jnp.max(lgm, axis=-1, keepdims=True)
    pg_sel = 1.0 / jnp.sum(jnp.exp(lgm - mg), axis=-1, keepdims=True)
    g_sel = jnp.min(jnp.where(lgm == mg, lane, far), axis=-1, keepdims=True)
    lo = N_GROUPS + EXPERTS_PER_GROUP * g_sel
    in_group = jnp.logical_and(lane >= lo, lane < lo + EXPERTS_PER_GROUP)
    lem = jnp.where(in_group, lg, -jnp.inf)
    me = jnp.max(lem, axis=-1, keepdims=True)
    ee = jnp.exp(lem - me)
    pe = ee / jnp.sum(ee, axis=-1, keepdims=True)
    pe = jnp.where(in_group, pe, -1.0)
    p1 = jnp.max(pe, axis=-1, keepdims=True)
    i1 = jnp.min(jnp.where(pe == p1, lane, far), axis=-1, keepdims=True)
    pe2 = jnp.where(lane == i1, -1.0, pe)
    p2 = jnp.max(pe2, axis=-1, keepdims=True)
    i2 = jnp.min(jnp.where(pe2 == p2, lane, far), axis=-1, keepdims=True)
    norm = pg_sel / (p1 + p2)
    e_ref[...] = jnp.where(lane == 0, i1 - N_GROUPS, jnp.where(lane == 1, i2 - N_GROUPS, 0))
    gate_ref[...] = jnp.where(lane == 0, p1 * norm, jnp.where(lane == 1, p2 * norm, 0.0))


def router(h, g, w, b):
    n, d = h.shape
    tm = TM_ROW
    row = lambda i: (i, 0)
    fixed = lambda i: (0, 0)
    return pl.pallas_call(
        _router_body,
        grid=(n // tm,),
        in_specs=[
            pl.BlockSpec((tm, d), row),
            pl.BlockSpec((1, d), fixed),
            pl.BlockSpec((d, LANES), fixed),
            pl.BlockSpec((1, LANES), fixed),
        ],
        out_specs=[pl.BlockSpec((tm, d), row), pl.BlockSpec((tm, LANES), row), pl.BlockSpec((tm, LANES), row)],
        out_shape=[
            jax.ShapeDtypeStruct((n, d), F32),
            jax.ShapeDtypeStruct((n, LANES), jnp.int32),
            jax.ShapeDtypeStruct((n, LANES), F32),
        ],
        compiler_params=_cparams(("parallel",)),
        name="router",
    )(h, g, w, b)


def _gather_body(idx_ref, src_ref, out_ref, sem):
    base = pl.program_id(0) * GATHER_ROWS

    def row_copy(r, src_row):
        return pltpu.make_async_copy(src_ref.at[pl.ds(src_row, 1)], out_ref.at[pl.ds(r, 1)], sem)

    def start(r, carry):
        row_copy(r, idx_ref[base + r]).start()
        return carry

    def wait(r, carry):
        row_copy(r, 0).wait()
        return carry

    lax.fori_loop(0, GATHER_ROWS, start, 0)
    lax.fori_loop(0, GATHER_ROWS, wait, 0)


def gather_rows(src, idx):
    n_out = idx.shape[0]
    width = src.shape[1]
    grid_spec = pltpu.PrefetchScalarGridSpec(
        num_scalar_prefetch=1,
        grid=(n_out // GATHER_ROWS,),
        in_specs=[pl.BlockSpec(memory_space=pl.ANY)],
        out_specs=pl.BlockSpec((GATHER_ROWS, width), lambda i, idx: (i, 0)),
        scratch_shapes=[pltpu.SemaphoreType.DMA(())],
    )
    return pl.pallas_call(
        _gather_body,
        grid_spec=grid_spec,
        out_shape=jax.ShapeDtypeStruct((n_out, width), src.dtype),
        compiler_params=_cparams(("arbitrary",)),
        name="gather_rows",
    )(idx, src)


def _experts_body(be_ref, nu_ref, x_ref, w1_ref, w3_ref, w2_ref, y_ref, w1b, w3b, w2b):
    b = pl.program_id(0)
    changed = jnp.logical_or(b == 0, be_ref[b] != be_ref[jnp.maximum(b - 1, 0)])

    @pl.when(changed)
    def _():
        w1b[...] = w1_ref[0, 0].astype(BF16)
        w3b[...] = w3_ref[0, 0].astype(BF16)
        w2b[...] = w2_ref[0, 0].astype(BF16)

    @pl.when(b < nu_ref[0])
    def _():
        x = x_ref[...].astype(BF16)
        a = jnp.dot(x, w1b[...], preferred_element_type=F32)
        g = jnp.dot(x, w3b[...], preferred_element_type=F32)
        mid = (a * _sigmoid(a) * g).astype(BF16)
        y_ref[...] = jnp.dot(mid, w2b[...], preferred_element_type=F32)

    @pl.when(b >= nu_ref[0])
    def _():
        y_ref[...] = jnp.zeros(y_ref.shape, F32)


def experts_ffn(block_expert, n_used, x_pad, w1, w3, w2, layer):
    n_slots, d = x_pad.shape
    n_blocks = n_slots // MOE_BM
    grid_spec = pltpu.PrefetchScalarGridSpec(
        num_scalar_prefetch=2,
        grid=(n_blocks,),
        in_specs=[
            pl.BlockSpec((MOE_BM, d), lambda b, be, nu: (b, 0)),
            pl.BlockSpec((1, 1, d, D_EXPERT), lambda b, be, nu: (layer, be[b], 0, 0)),
            pl.BlockSpec((1, 1, d, D_EXPERT), lambda b, be, nu: (layer, be[b], 0, 0)),
            pl.BlockSpec((1, 1, D_EXPERT, d), lambda b, be, nu: (layer, be[b], 0, 0)),
        ],
        out_specs=pl.BlockSpec((MOE_BM, d), lambda b, be, nu: (b, 0)),
        scratch_shapes=[
            pltpu.VMEM((d, D_EXPERT), BF16),
            pltpu.VMEM((d, D_EXPERT), BF16),
            pltpu.VMEM((D_EXPERT, d), BF16),
        ],
    )
    return pl.pallas_call(
        _experts_body,
        grid_spec=grid_spec,
        out_shape=jax.ShapeDtypeStruct((n_slots, d), F32),
        compiler_params=_cparams(("arbitrary",)),
        name="experts_ffn",
    )(block_expert, n_used, x_pad, w1, w3, w2)


def moe_dispatch(experts):
    a = N_TOK * TOP_K
    n_blocks = a // MOE_BM + N_EXPERTS
    flat_e = experts.reshape(a)
    order = jnp.argsort(flat_e)
    sorted_e = flat_e[order]
    counts = jnp.bincount(flat_e, length=N_EXPERTS)
    padded = (counts + MOE_BM - 1) // MOE_BM * MOE_BM
    start = jnp.cumsum(counts) - counts
    pad_end = jnp.cumsum(padded)
    pad_start = pad_end - padded
    slot_sorted = (pad_start[sorted_e] + jnp.arange(a) - start[sorted_e]).astype(jnp.int32)
    slot = jnp.zeros((a,), jnp.int32).at[order].set(slot_sorted)
    n_slots = n_blocks * MOE_BM
    token_of_slot = (jnp.arange(n_slots, dtype=jnp.int32) % N_TOK).at[slot].set(
        jnp.arange(a, dtype=jnp.int32) // TOP_K)
    block_expert = jnp.minimum(
        jnp.searchsorted(pad_end, jnp.arange(n_blocks) * MOE_BM, side='right'), N_EXPERTS - 1).astype(jnp.int32)
    n_used = (pad_end[-1:] // MOE_BM).astype(jnp.int32)
    return slot, token_of_slot, block_expert, n_used


def _ple_body(h_ref, y0_ref, y1_ref, gate_ref, g_ref, wg_ref, p_ref, wp_ref, gf_ref, out_ref, *, final):
    gt = _round_bf16(gate_ref[...])
    h2 = h_ref[...] + (gt[:, 0:1] * _round_bf16(y0_ref[...]) + gt[:, 1:2] * _round_bf16(y1_ref[...]))
    xn = _rms(h2, g_ref[...]).astype(BF16)
    gate = _sigmoid(jnp.dot(xn, wg_ref[...], preferred_element_type=F32))
    emb = jnp.dot(p_ref[...].astype(BF16), wp_ref[...], preferred_element_type=F32)
    h3 = h2 + gate * emb
    out_ref[...] = _rms(h3, gf_ref[...]) if final else h3


def ple(h, y_kmajor, gates, g, wg, p, wp, g_final, *, final):
    n, d = h.shape
    tm = TM_ROW
    row = lambda i: (i, 0)
    fixed = lambda i: (0, 0)
    return pl.pallas_call(
        functools.partial(_ple_body, final=final),
        grid=(n // tm,),
        in_specs=[
            pl.BlockSpec((tm, d), row),
            pl.BlockSpec((tm, d), row),
            pl.BlockSpec((tm, d), lambda i: (i + n // tm, 0)),
            pl.BlockSpec((tm, LANES), row),
            pl.BlockSpec((1, d), fixed),
            pl.BlockSpec((d, d), fixed),
            pl.BlockSpec((tm, PLE_DIM), row),
            pl.BlockSpec((PLE_DIM, d), fixed),
            pl.BlockSpec((1, d), fixed),
        ],
        out_specs=pl.BlockSpec((tm, d), row),
        out_shape=jax.ShapeDtypeStruct((n, d), F32),
        compiler_params=_cparams(("parallel",)),
        name="ple",
    )(h, y_kmajor, y_kmajor, gates, g, wg, p, wp, g_final)


def _pad_cols(w, width):
    return jnp.pad(w, ((0, 0), (0, width - w.shape[1])))


def _rope_tables():
    pos = jnp.concatenate([jnp.tile(jnp.arange(SEQ), BATCH), jnp.tile(PAST_LEN + jnp.arange(DEC_SEQ), DEC_BATCH)])
    inv_freq = ROPE_THETA ** (-jnp.arange(0, QK_ROPE, 2, dtype=F32) / QK_ROPE)
    ang = pos.astype(F32)[:, None] * inv_freq[None, :]
    cos, sin = jnp.cos(ang), jnp.sin(ang)
    cc = jnp.concatenate([cos, cos], axis=1)
    ss = jnp.concatenate([-sin, sin], axis=1)
    return jnp.tile(cc, (1, LANES // QK_ROPE)), jnp.tile(ss, (1, LANES // QK_ROPE))


def _swap_halves(w):
    half = w.shape[-1] // 2
    return jnp.concatenate([w[..., half:], w[..., :half]], axis=-1)


def _mlstm_layer(h, j, ln, w_in, b_if, g_head, w_out, state_c, state_n, state_m):
    w_main = w_in[:, :MLSTM_MAIN].astype(BF16)
    w_gate = _pad_cols(w_in[:, MLSTM_MAIN:], LANES).astype(BF16)
    qkv, o_gate, gates = norm_proj(h, ln.reshape(1, D_MODEL), w_main, w_gate)
    bias = _pad_cols(b_if.reshape(1, 2 * MLSTM_HEADS), LANES)
    gh = g_head.reshape(1, D_MODEL)

    zc = jnp.zeros((BATCH, MLSTM_HEADS, MLSTM_DK, MLSTM_DV), F32)
    zn = jnp.zeros((BATCH, MLSTM_HEADS, MLSTM_DK), F32)
    zm = jnp.zeros((BATCH, 1, LANES), F32)
    gated_p, c_p, n_p, m_p = mlstm_scan(qkv, o_gate, gates, bias, gh, zc, zn, zm, n_seq=BATCH,
                                        n_chunks=SEQ // MLSTM_CHUNK_P, chunk=MLSTM_CHUNK_P, valid=MLSTM_CHUNK_P)

    def pad_rows(a):
        a = a[N_PROMPT:].reshape(DEC_BATCH, DEC_SEQ, a.shape[1])
        return jnp.pad(a, ((0, 0), (0, MLSTM_CHUNK_S - DEC_SEQ), (0, 0))).reshape(-1, a.shape[2])

    m0 = _pad_cols(state_m[j], LANES).reshape(DEC_BATCH, 1, LANES)
    gated_s, c_s, n_s, m_s = mlstm_scan(pad_rows(qkv), pad_rows(o_gate), pad_rows(gates), bias, gh, state_c[j],
                                        state_n[j], m0, n_seq=DEC_BATCH, n_chunks=1, chunk=MLSTM_CHUNK_S,
                                        valid=DEC_SEQ)
    gated_s = gated_s.reshape(DEC_BATCH, MLSTM_CHUNK_S, D_MODEL)[:, :DEC_SEQ].reshape(N_SAMPLE, D_MODEL)

    w_o = w_out.astype(BF16)
    h = linear_res(h, gated_p, w_o, row_block_offset=0)
    h = linear_res(h, gated_s, w_o, row_block_offset=N_PROMPT // TM)
    state_p = (c_p, n_p, m_p[:, 0, :MLSTM_HEADS])
    state_s = (c_s, n_s, m_s[:, 0, :MLSTM_HEADS])
    return h, state_p, state_s


def _mla_layer(h, j, ln, w_a, g_q, w_qb, g_kv, w_kvb, w_o, cc, ss, cache_ckv, cache_kr, page_table):
    w_kr = w_a[:, Q_LORA + KV_LORA:]
    w_a_ext = jnp.concatenate([w_a, _swap_halves(w_kr)], axis=1).astype(BF16)
    cq, ckv, kr, k_wide = mla_a(h, ln.reshape(1, D_MODEL), w_a_ext, g_q.reshape(1, Q_LORA),
                                g_kv.reshape(1, KV_LORA), cc, ss)

    wq = w_qb.reshape(Q_LORA, MLA_HEADS, QK_NOPE + QK_ROPE)
    w_nope = wq[:, :, :QK_NOPE].reshape(Q_LORA, MLA_HEADS * QK_NOPE).astype(BF16)
    w_rope = wq[:, :, QK_NOPE:]
    pad = ((0, 0), (0, 0), (0, LANES - QK_ROPE))
    w_r = jnp.pad(w_rope, pad).reshape(Q_LORA, MLA_HEADS * LANES).astype(BF16)
    w_rs = jnp.pad(_swap_halves(w_rope), pad).reshape(Q_LORA, MLA_HEADS * LANES).astype(BF16)
    wkv = w_kvb.reshape(KV_LORA, MLA_HEADS, QK_NOPE + V_HEAD)
    wk_t = jnp.transpose(wkv[:, :, :QK_NOPE], (1, 2, 0)).astype(BF16)
    wv = jnp.transpose(wkv[:, :, QK_NOPE:], (1, 0, 2)).astype(BF16)
    q_wide = mla_q(cq, w_nope, w_r, w_rs, wk_t, cc, ss)

    o_lat_p = flash_attention(q_wide, k_wide)

    q_s = q_wide[N_PROMPT:].reshape(DEC_BATCH, DEC_SEQ, MLA_HEADS, QK_WIDE)
    q_s = jnp.transpose(q_s, (0, 2, 1, 3)).reshape(DEC_BATCH, MLA_HEADS * DEC_SEQ, QK_WIDE)
    k_new = jnp.pad(k_wide[N_PROMPT:].reshape(DEC_BATCH, DEC_SEQ, QK_WIDE),
                    ((0, 0), (0, MLSTM_CHUNK_S - DEC_SEQ), (0, 0)))
    o_s = paged_attention(page_table, q_s, k_new, cache_ckv, jnp.swapaxes(cache_kr, 2, 3), j)
    o_lat_s = jnp.transpose(o_s.reshape(DEC_BATCH, MLA_HEADS, DEC_SEQ, KV_LORA), (0, 2, 1, 3))
    o_lat_s = o_lat_s.reshape(N_SAMPLE, MLA_HEADS * KV_LORA)

    wo = w_o.astype(BF16)
    h = mla_out(h, o_lat_p, wv, wo, row_block_offset=0)
    h = mla_out(h, o_lat_s, wv, wo, row_block_offset=N_PROMPT // TM_ROW)
    return h, ckv, kr


def _moe_ple_layer(h, i, ln_ffn, w_rg, b_rg, w_re, b_re, w1, w3, w2, p_all, ple_g, ple_wg, ple_wp, ln_final):
    w_r = _pad_cols(jnp.concatenate([w_rg, w_re], axis=1), LANES).astype(BF16)
    b_r = _pad_cols(jnp.concatenate([b_rg, b_re]).reshape(1, -1), LANES)
    u, e_idx, gates = router(h, ln_ffn.reshape(1, D_MODEL), w_r, b_r)
    slot, token_of_slot, block_expert, n_used = moe_dispatch(e_idx[:, :TOP_K])
    x_pad = gather_rows(u, token_of_slot)
    y_pad = experts_ffn(block_expert, n_used, x_pad, w1, w3, w2, i)
    y_kmajor = gather_rows(y_pad, slot.reshape(N_TOK, TOP_K).T.reshape(-1))
    return ple(h, y_kmajor, gates, ple_g.reshape(1, D_MODEL), ple_wg.astype(BF16), p_all, ple_wp.astype(BF16),
               ln_final.reshape(1, D_MODEL), final=(i == DEPTH - 1))


def kernel(x_prompt, x_sample, cache_mla_ckv, cache_mla_krope, state_mlstm_C, state_mlstm_n, state_mlstm_m,
           page_table, p_prompt, p_sample, ln_mix, ln_ffn, ln_final, mlstm_w_in, mlstm_b_if, mlstm_g_head,
           mlstm_w_out, mla_w_a, mla_g_q, mla_w_qb, mla_g_kv, mla_w_kvb, mla_w_o, moe_w_rg, moe_b_rg, moe_w_re,
           moe_b_re, moe_w1, moe_w3, moe_w2, ple_g, ple_w_gate, ple_w_proj):
    h = jnp.concatenate([x_prompt.reshape(N_PROMPT, D_MODEL), x_sample.reshape(N_SAMPLE, D_MODEL)], axis=0)
    cc, ss = _rope_tables()
    ckv_rows, kr_rows, states_p, states_s = [], [], [], []
    for i in range(DEPTH):
        j = i // 2
        if i % 2 == 0:
            h, st_p, st_s = _mlstm_layer(h, j, ln_mix[i], mlstm_w_in[j], mlstm_b_if[j], mlstm_g_head[j],
                                         mlstm_w_out[j], state_mlstm_C, state_mlstm_n, state_mlstm_m)
            states_p.append(st_p)
            states_s.append(st_s)
        else:
            h, ckv, kr = _mla_layer(h, j, ln_mix[i], mla_w_a[j], mla_g_q[j], mla_w_qb[j], mla_g_kv[j],
                                    mla_w_kvb[j], mla_w_o[j], cc, ss, cache_mla_ckv, cache_mla_krope, page_table)
            ckv_rows.append(ckv)
            kr_rows.append(kr)
        p_all = jnp.concatenate([p_prompt[i].reshape(N_PROMPT, PLE_DIM), p_sample[i].reshape(N_SAMPLE, PLE_DIM)])
        h = _moe_ple_layer(h, i, ln_ffn[i], moe_w_rg[i], moe_b_rg[i], moe_w_re[i], moe_b_re[i], moe_w1, moe_w3,
                           moe_w2, p_all, ple_g[i], ple_w_gate[i], ple_w_proj[i], ln_final)

    ckv_all = jnp.stack(ckv_rows)
    kr_all = jnp.stack(kr_rows)

    def split(a, width):
        return (a[:, :N_PROMPT].reshape(-1, BATCH, SEQ, width), a[:, N_PROMPT:].reshape(-1, DEC_BATCH, DEC_SEQ, width))

    ckv_p, ckv_s = split(ckv_all, KV_LORA)
    kr_p, kr_s = split(kr_all, QK_ROPE)
    c_p, n_p, m_p = (jnp.stack(t) for t in zip(*states_p))
    c_s, n_s, m_s = (jnp.stack(t) for t in zip(*states_s))
    y_prompt = h[:N_PROMPT].reshape(BATCH, SEQ, D_MODEL)
    y_sample = h[N_PROMPT:].reshape(DEC_BATCH, DEC_SEQ, D_MODEL)
    return (y_prompt, y_sample, ckv_p, kr_p, c_p, n_p, m_p, ckv_s, kr_s, c_s, n_s, m_s)
```

```python
import functools

import jax
import jax.numpy as jnp
from jax import lax
from jax.experimental import pallas as pl
from jax.experimental.pallas import tpu as pltpu

F32 = jnp.float32
BF16 = jnp.bfloat16
HIGHEST = lax.Precision.HIGHEST

D_MODEL = 2048
BATCH = 2
SEQ = 4096
DEPTH = 4
DEC_BATCH = 128
DEC_SEQ = 4
PAST_LEN = 8192
PAGE_SIZE = 128
N_PAGES = PAST_LEN // PAGE_SIZE
EPS = 1e-6

MLSTM_HEADS = 8
MLSTM_DV = D_MODEL // MLSTM_HEADS
MLSTM_DK = MLSTM_DV // 2
MLSTM_QK = MLSTM_HEADS * MLSTM_DK
GATE_SOFTCAP = 15.0
MLSTM_MAIN = 2 * MLSTM_QK + 2 * D_MODEL
QKV_TILES = (2 * MLSTM_QK + D_MODEL) // MLSTM_QK

MLA_HEADS = 16
QK_NOPE = 128
QK_ROPE = 64
V_HEAD = 128
Q_LORA = 512
KV_LORA = 512
ROPE_THETA = 10000.0
ATTN_SCALE = (QK_NOPE + QK_ROPE) ** -0.5
QK_WIDE = KV_LORA + 128

N_GROUPS = 8
EXPERTS_PER_GROUP = 8
N_EXPERTS = N_GROUPS * EXPERTS_PER_GROUP
TOP_K = 2
D_EXPERT = 512
PLE_DIM = 256

N_PROMPT = BATCH * SEQ
N_SAMPLE = DEC_BATCH * DEC_SEQ
N_TOK = N_PROMPT + N_SAMPLE

LANES = 128
NEG = -1e30
VMEM_LIMIT = 56 * 1024 * 1024

TM = 512
TM_ROW = 256
MLSTM_CHUNK_P = 64
MLSTM_CHUNK_S = 16
TQ = 256
TK = 1024
Q_PER_K = TK // TQ
FLASH_ROWS = 512
EXP2_SCALE = ATTN_SCALE * 1.4426950408889634
PAGES_PER_STEP = 32
Q_HEADS_PER_STEP = 4
MOE_BM = 256
GATHER_ROWS = 256


def _cparams(sem):
    return pltpu.CompilerParams(dimension_semantics=sem, vmem_limit_bytes=VMEM_LIMIT)


def _rms(x, g):
    return x * lax.rsqrt(jnp.mean(x * x, axis=-1, keepdims=True) + EPS) * g


def _sigmoid(x):
    return 1.0 / (1.0 + jnp.exp(-x))


def _round_bf16(x):
    return x.astype(BF16).astype(F32)


def _norm_proj_body(h_ref, g_ref, w_ref, wg_ref, qkv_ref, o_ref, gate_ref, xn_ref):
    j = pl.program_id(1)

    @pl.when(j == 0)
    def _():
        xn = _rms(h_ref[...], g_ref[...]).astype(BF16)
        xn_ref[...] = xn
        gate_ref[...] = jnp.dot(xn, wg_ref[...], preferred_element_type=F32)

    acc = jnp.dot(xn_ref[...], w_ref[...], preferred_element_type=F32)

    @pl.when(j == 0)
    def _():
        qkv_ref[...] = (acc * MLSTM_DK ** -0.5).astype(BF16)

    @pl.when(jnp.logical_and(j > 0, j < QKV_TILES))
    def _():
        qkv_ref[...] = acc.astype(BF16)

    @pl.when(j >= QKV_TILES)
    def _():
        o_ref[...] = acc


def norm_proj(h, g, w, wg):
    n, d = h.shape
    tn = MLSTM_QK
    return pl.pallas_call(
        _norm_proj_body,
        grid=(n // TM, MLSTM_MAIN // tn),
        in_specs=[
            pl.BlockSpec((TM, d), lambda i, j: (i, 0)),
            pl.BlockSpec((1, d), lambda i, j: (0, 0)),
            pl.BlockSpec((d, tn), lambda i, j: (0, j)),
            pl.BlockSpec((d, LANES), lambda i, j: (0, 0)),
        ],
        out_specs=[
            pl.BlockSpec((TM, tn), lambda i, j: (i, jnp.minimum(j, QKV_TILES - 1))),
            pl.BlockSpec((TM, tn), lambda i, j: (i, jnp.maximum(j - QKV_TILES, 0))),
            pl.BlockSpec((TM, LANES), lambda i, j: (i, 0)),
        ],
        out_shape=[
            jax.ShapeDtypeStruct((n, 2 * MLSTM_QK + D_MODEL), BF16),
            jax.ShapeDtypeStruct((n, D_MODEL), F32),
            jax.ShapeDtypeStruct((n, LANES), F32),
        ],
        scratch_shapes=[pltpu.VMEM((TM, d), BF16)],
        compiler_params=_cparams(("parallel", "arbitrary")),
        name="norm_proj",
    )(h, g, w, wg)


def _linear_res_body(x_ref, w_ref, r_ref, o_ref):
    o_ref[...] = r_ref[...] + jnp.dot(x_ref[...], w_ref[...], preferred_element_type=F32)


def linear_res(h, x, w, *, row_block_offset, tn=1024):
    m, k = x.shape
    d = w.shape[1]
    return pl.pallas_call(
        _linear_res_body,
        grid=(m // TM, d // tn),
        in_specs=[
            pl.BlockSpec((TM, k), lambda i, j: (i, 0)),
            pl.BlockSpec((k, tn), lambda i, j: (0, j)),
            pl.BlockSpec((TM, tn), lambda i, j: (i + row_block_offset, j)),
        ],
        out_specs=pl.BlockSpec((TM, tn), lambda i, j: (i + row_block_offset, j)),
        out_shape=jax.ShapeDtypeStruct(h.shape, F32),
        input_output_aliases={2: 0},
        compiler_params=_cparams(("parallel", "parallel")),
        name="linear_res",
    )(x, w, h)


def _mlstm_body(q_ref, k_ref, v_ref, o_ref, gt_ref, bif_ref, gh_ref, c0_ref, n0_ref, m0_ref, *rest, chunk, valid):
    out_ref, c_ref, n_ref, m_ref = rest[-4:]
    L = chunk

    @pl.when(pl.program_id(1) == 0)
    def _():
        c_ref[...] = c0_ref[...]
        n_ref[...] = n0_ref[...]
        m_ref[...] = m0_ref[...]

    gc = GATE_SOFTCAP * jnp.tanh((gt_ref[...] + bif_ref[...]) / GATE_SOFTCAP)
    logf_all = jnp.minimum(gc, 0.0) - jnp.log1p(jnp.exp(-jnp.abs(gc)))
    logi_all = gc
    if valid < L:
        live = lax.broadcasted_iota(jnp.int32, (L, 1), 0) < valid
        logi_all = jnp.where(live, gc, NEG)
        logf_all = jnp.where(live, logf_all, 0.0)
    rows = lax.broadcasted_iota(jnp.int32, (L, L), 0)
    cols = lax.broadcasted_iota(jnp.int32, (L, L), 1)
    causal = rows >= cols
    eye = rows == cols
    b_all = jnp.dot(causal.astype(F32), logf_all, precision=HIGHEST, preferred_element_type=F32)

    def as_row(col):
        return jnp.sum(jnp.where(eye, col, 0.0), axis=0, keepdims=True)

    for hd in range(MLSTM_HEADS):
        b_col = b_all[:, MLSTM_HEADS + hd:MLSTM_HEADS + hd + 1]
        li_col = logi_all[:, hd:hd + 1]
        b_row = as_row(b_col)
        li_row = as_row(li_col)
        m0 = m_ref[0, :, hd:hd + 1]
        d = jnp.where(causal, b_col - b_row + li_row, NEG)
        carry = b_col + m0
        m_row = jnp.maximum(carry, jnp.max(d, axis=-1, keepdims=True))
        q = q_ref[:, hd * MLSTM_DK:(hd + 1) * MLSTM_DK]
        k = k_ref[:, hd * MLSTM_DK:(hd + 1) * MLSTM_DK]
        v = v_ref[:, hd * MLSTM_DV:(hd + 1) * MLSTM_DV]
        qk = lax.dot_general(q, k, (((1,), (1,)), ((), ())), preferred_element_type=F32)
        s = qk * jnp.exp(d - m_row)
        w_prev = jnp.exp(carry - m_row)
        c_old = c_ref[0, 0, hd]
        n_old = n_ref[0, hd:hd + 1, :]
        num = (jnp.dot(s.astype(BF16), v, preferred_element_type=F32)
               + w_prev * jnp.dot(q, c_old.astype(BF16), preferred_element_type=F32))
        den = (jnp.sum(s, axis=-1, keepdims=True)
               + w_prev * jnp.sum(q.astype(F32) * _round_bf16(n_old), axis=-1, keepdims=True))
        hh = num / jnp.maximum(jnp.abs(den), jnp.exp(-m_row))

        b_end = b_col[L - 1:L, :]
        dend = b_end - b_col + li_col
        m_new = jnp.maximum(b_end + m0, jnp.max(dend, axis=0, keepdims=True))
        a_col = jnp.exp(dend - m_new)
        keep = jnp.exp(b_end + m0 - m_new)
        kf = k.astype(F32)
        upd = lax.dot_general((a_col * kf).astype(BF16), v, (((0,), (0,)), ((), ())), preferred_element_type=F32)
        c_ref[0, 0, hd] = keep * c_old + upd
        n_ref[0, hd:hd + 1, :] = keep * n_old + jnp.sum(_round_bf16(a_col) * kf, axis=0, keepdims=True)
        m_ref[0, :, hd:hd + 1] = m_new

        sl = slice(hd * MLSTM_DV, (hd + 1) * MLSTM_DV)
        hn = hh * lax.rsqrt(jnp.mean(hh * hh, axis=-1, keepdims=True) + EPS) * gh_ref[:, sl]
        out_ref[:, sl] = (_sigmoid(o_ref[:, sl]) * hn).astype(BF16)


def mlstm_scan(qkv, o_gate, gates, b_if, g_head, c0, n0, m0, *, n_seq, n_chunks, chunk, valid, layer=0,
               c_other=None):
    rows = n_seq * n_chunks * chunk
    row = lambda s, c: s * n_chunks + c
    body = functools.partial(_mlstm_body, chunk=chunk, valid=valid)
    c_block = pl.BlockSpec((1, 1, MLSTM_HEADS, MLSTM_DK, MLSTM_DV), lambda s, c: (layer, s, 0, 0, 0))
    extra_specs, extra_args, aliases = [], [], {}
    if c_other is not None:
        extra_specs, extra_args, aliases = [pl.BlockSpec(memory_space=pl.ANY)], [c_other], {10: 1}
    return pl.pallas_call(
        body,
        grid=(n_seq, n_chunks),
        input_output_aliases=aliases,
        in_specs=[
            pl.BlockSpec((chunk, MLSTM_QK), lambda s, c: (row(s, c), 0)),
            pl.BlockSpec((chunk, MLSTM_QK), lambda s, c: (row(s, c), 1)),
            pl.BlockSpec((chunk, D_MODEL), lambda s, c: (row(s, c), 1)),
            pl.BlockSpec((chunk, D_MODEL), lambda s, c: (row(s, c), 0)),
            pl.BlockSpec((chunk, LANES), lambda s, c: (row(s, c), 0)),
            pl.BlockSpec((1, LANES), lambda s, c: (0, 0)),
            pl.BlockSpec((1, D_MODEL), lambda s, c: (0, 0)),
            c_block,
            pl.BlockSpec((1, MLSTM_HEADS, MLSTM_DK), lambda s, c: (s, 0, 0)),
            pl.BlockSpec((1, 1, LANES), lambda s, c: (s, 0, 0)),
        ] + extra_specs,
        out_specs=[
            pl.BlockSpec((chunk, D_MODEL), lambda s, c: (row(s, c), 0)),
            c_block,
            pl.BlockSpec((1, MLSTM_HEADS, MLSTM_DK), lambda s, c: (s, 0, 0)),
            pl.BlockSpec((1, 1, LANES), lambda s, c: (s, 0, 0)),
        ],
        out_shape=[
            jax.ShapeDtypeStruct((rows, D_MODEL), BF16),
            jax.ShapeDtypeStruct(c0.shape, F32),
            jax.ShapeDtypeStruct((n_seq, MLSTM_HEADS, MLSTM_DK), F32),
            jax.ShapeDtypeStruct((n_seq, 1, LANES), F32),
        ],
        compiler_params=_cparams(("parallel", "arbitrary")),
        name="mlstm_scan",
    )(qkv, qkv, qkv, o_gate, gates, b_if, g_head, c0, n0, m0, *extra_args)


def _mla_a_body(h_ref, g_ref, w_ref, gq_ref, gkv_ref, cc_ref, ss_ref, cq_ref, ckv_ref, kr_ref, kw_ref):
    xn = _rms(h_ref[...], g_ref[...]).astype(BF16)
    a = jnp.dot(xn, w_ref[...], preferred_element_type=F32)
    cq_ref[...] = _rms(a[:, :Q_LORA], gq_ref[...]).astype(BF16)
    ckv = _rms(a[:, Q_LORA:Q_LORA + KV_LORA], gkv_ref[...])
    ckv_ref[...] = ckv
    t = a[:, Q_LORA + KV_LORA:]
    kr = t[:, :QK_ROPE] * cc_ref[:, :QK_ROPE] + t[:, QK_ROPE:] * ss_ref[:, :QK_ROPE]
    kr_ref[...] = kr
    kw_ref[:, :KV_LORA] = ckv.astype(BF16)
    kw_ref[:, KV_LORA:] = jnp.concatenate([kr, jnp.zeros_like(kr)], axis=1).astype(BF16)


def mla_a(h, g, w, gq, gkv, cc, ss):
    n, d = h.shape
    wa = w.shape[1]
    tm = TM_ROW
    row = lambda i: (i, 0)
    fixed = lambda i: (0, 0)
    return pl.pallas_call(
        _mla_a_body,
        grid=(n // tm,),
        in_specs=[
            pl.BlockSpec((tm, d), row),
            pl.BlockSpec((1, d), fixed),
            pl.BlockSpec((d, wa), fixed),
            pl.BlockSpec((1, Q_LORA), fixed),
            pl.BlockSpec((1, KV_LORA), fixed),
            pl.BlockSpec((tm, LANES), row),
            pl.BlockSpec((tm, LANES), row),
        ],
        out_specs=[
            pl.BlockSpec((tm, Q_LORA), row),
            pl.BlockSpec((tm, KV_LORA), row),
            pl.BlockSpec((tm, QK_ROPE), row),
            pl.BlockSpec((tm, QK_WIDE), row),
        ],
        out_shape=[
            jax.ShapeDtypeStruct((n, Q_LORA), BF16),
            jax.ShapeDtypeStruct((n, KV_LORA), F32),
            jax.ShapeDtypeStruct((n, QK_ROPE), F32),
            jax.ShapeDtypeStruct((n, QK_WIDE), BF16),
        ],
        compiler_params=_cparams(("parallel",)),
        name="mla_a",
    )(h, g, w, gq, gkv, cc, ss)


def _mla_q_body(c_ref, wn_ref, wr_ref, wrs_ref, wk_ref, cc_ref, ss_ref, out_ref):
    c = c_ref[...]
    qn = jnp.dot(c, wn_ref[...], preferred_element_type=F32).astype(BF16)
    r1 = jnp.dot(c, wr_ref[...], preferred_element_type=F32)
    r2 = jnp.dot(c, wrs_ref[...], preferred_element_type=F32)
    cc = cc_ref[...]
    ss = ss_ref[...]
    for h in range(Q_HEADS_PER_STEP):
        hs = slice(h * LANES, (h + 1) * LANES)
        qlat = jnp.dot(qn[:, hs], wk_ref[h], preferred_element_type=F32)
        out_ref[:, h * QK_WIDE:h * QK_WIDE + KV_LORA] = qlat.astype(BF16)
        out_ref[:, h * QK_WIDE + KV_LORA:(h + 1) * QK_WIDE] = (r1[:, hs] * cc + r2[:, hs] * ss).astype(BF16)


def mla_q(cq, wn, wr, wrs, wk_t, cc, ss):
    n = cq.shape[0]
    hp = Q_HEADS_PER_STEP
    head_cols = lambda i, h: (0, h)
    return pl.pallas_call(
        _mla_q_body,
        grid=(n // TM, MLA_HEADS // hp),
        in_specs=[
            pl.BlockSpec((TM, Q_LORA), lambda i, h: (i, 0)),
            pl.BlockSpec((Q_LORA, hp * QK_NOPE), head_cols),
            pl.BlockSpec((Q_LORA, hp * LANES), head_cols),
            pl.BlockSpec((Q_LORA, hp * LANES), head_cols),
            pl.BlockSpec((hp, QK_NOPE, KV_LORA), lambda i, h: (h, 0, 0)),
            pl.BlockSpec((TM, LANES), lambda i, h: (i, 0)),
            pl.BlockSpec((TM, LANES), lambda i, h: (i, 0)),
        ],
        out_specs=pl.BlockSpec((TM, hp * QK_WIDE), lambda i, h: (i, h)),
        out_shape=jax.ShapeDtypeStruct((n, MLA_HEADS * QK_WIDE), BF16),
        compiler_params=_cparams(("parallel", "arbitrary")),
        name="mla_q",
    )(cq, wn, wr, wrs, wk_t, cc, ss)


def _flash_body(q_ref, k_ref, o_ref, qs_ref, acc_ref, m_ref, l_ref):
    qi = pl.program_id(1)
    ki = pl.program_id(2)
    rows_all = MLA_HEADS * TQ
    kd = qi // Q_PER_K
    n_tiles = TK // LANES
    n_acc = KV_LORA // LANES

    @pl.when(ki == 0)
    def _():
        for h in range(MLA_HEADS):
            qs_ref[h * TQ:(h + 1) * TQ, :] = q_ref[:, h * QK_WIDE:(h + 1) * QK_WIDE]
        m_ref[...] = jnp.full(m_ref.shape, -jnp.inf, F32)
        l_ref[...] = jnp.zeros(l_ref.shape, F32)
        acc_ref[...] = jnp.zeros(acc_ref.shape, F32)

    def update(masked):
        kb = k_ref[...]
        vb = kb[:, :KV_LORA]
        for r0 in range(0, rows_all, FLASH_ROWS):
            rs = slice(r0, r0 + FLASH_ROWS)
            s = lax.dot_general(qs_ref[rs, :], kb, (((1,), (1,)), ((), ())), preferred_element_type=F32)
            if masked:
                tok = (lax.broadcasted_iota(jnp.int32, s.shape, 0) & (TQ - 1)) + (qi - kd * Q_PER_K) * TQ
                key = lax.broadcasted_iota(jnp.int32, s.shape, 1)
                s = jnp.where(key <= tok, s, -jnp.inf)
            tiles = [s[:, t * LANES:(t + 1) * LANES] for t in range(n_tiles)]
            tile_max = functools.reduce(jnp.maximum, tiles)
            m_prev = m_ref[rs, :]
            m_new = jnp.maximum(m_prev, jnp.max(tile_max, axis=-1, keepdims=True))
            ps = [jnp.exp2((t - m_new) * EXP2_SCALE) for t in tiles]
            alpha = jnp.exp2((m_prev - m_new) * EXP2_SCALE)
            row_sum = jnp.sum(functools.reduce(jnp.add, ps), axis=-1, keepdims=True)
            l_ref[rs, :] = alpha * l_ref[rs, :] + row_sum
            p = jnp.concatenate([t.astype(BF16) for t in ps], axis=1)
            pv = jnp.dot(p, vb, preferred_element_type=F32)
            for c in range(n_acc):
                cs = slice(c * LANES, (c + 1) * LANES)
                acc_ref[rs, cs] = alpha * acc_ref[rs, cs] + pv[:, cs]
            m_ref[rs, :] = m_new

    @pl.when(ki < kd)
    def _():
        update(False)

    @pl.when(ki == kd)
    def _():
        update(True)
        for h in range(MLA_HEADS):
            hs = slice(h * TQ, (h + 1) * TQ)
            inv = 1.0 / l_ref[hs, :]
            for c in range(n_acc):
                cs = slice(c * LANES, (c + 1) * LANES)
                o_ref[:, h * KV_LORA + c * LANES:h * KV_LORA + (c + 1) * LANES] = (acc_ref[hs, cs] * inv).astype(BF16)


def flash_attention(q_wide, k_wide):
    nq = SEQ // TQ
    nk = SEQ // TK
    rows_all = MLA_HEADS * TQ
    return pl.pallas_call(
        _flash_body,
        grid=(BATCH, nq, nk),
        in_specs=[
            pl.BlockSpec((TQ, MLA_HEADS * QK_WIDE), lambda b, qi, ki: (b * nq + qi, 0)),
            pl.BlockSpec((TK, QK_WIDE), lambda b, qi, ki: (b * nk + jnp.minimum(ki, qi // Q_PER_K), 0)),
        ],
        out_specs=pl.BlockSpec((TQ, MLA_HEADS * KV_LORA), lambda b, qi, ki: (b * nq + qi, 0)),
        out_shape=jax.ShapeDtypeStruct((N_PROMPT, MLA_HEADS * KV_LORA), BF16),
        scratch_shapes=[
            pltpu.VMEM((rows_all, QK_WIDE), BF16),
            pltpu.VMEM((rows_all, KV_LORA), F32),
            pltpu.VMEM((rows_all, LANES), F32),
            pltpu.VMEM((rows_all, LANES), F32),
        ],
        compiler_params=_cparams(("parallel", "parallel", "arbitrary")),
        name="flash_attention",
    )(q_wide, k_wide)


def _paged_body(pt_ref, q_ref, knew_ref, *rest):
    del pt_ref
    npg = PAGES_PER_STEP
    ckv_refs = rest[:npg]
    kr_refs = rest[npg:2 * npg]
    o_ref, acc_ref, m_ref, l_ref = rest[2 * npg:]
    step = pl.program_id(1)
    q = q_ref[0]
    q_lat = q[:, :KV_LORA]
    q_rope = q[:, KV_LORA:KV_LORA + QK_ROPE]
    nt = (((1,), (1,)), ((), ()))

    def online(parts, values):
        w = parts[0].shape[1]
        m_prev = m_ref[...]
        m_new = jnp.maximum(m_prev, jnp.max(functools.reduce(jnp.maximum, parts), axis=-1, keepdims=True))
        ps = [jnp.exp2((t - m_new[:, :w]) * EXP2_SCALE) for t in parts]
        alpha = jnp.exp2((m_prev - m_new) * EXP2_SCALE)
        l_ref[...] = alpha * l_ref[...] + jnp.sum(functools.reduce(jnp.add, ps), axis=-1, keepdims=True)
        pv = functools.reduce(jnp.add, [jnp.dot(t.astype(BF16), val, preferred_element_type=F32)
                                        for t, val in zip(ps, values)])
        for c in range(KV_LORA // LANES):
            cs = slice(c * LANES, (c + 1) * LANES)
            acc_ref[:, cs] = alpha * acc_ref[:, cs] + pv[:, cs]
        m_ref[...] = m_new

    @pl.when(step == 0)
    def _():
        m_ref[...] = jnp.full(m_ref.shape, -jnp.inf, F32)
        l_ref[...] = jnp.zeros(l_ref.shape, F32)
        acc_ref[...] = jnp.zeros(acc_ref.shape, F32)
        kn = knew_ref[0]
        s = lax.dot_general(q, kn, nt, preferred_element_type=F32)
        tok = lax.broadcasted_iota(jnp.int32, s.shape, 0) & (DEC_SEQ - 1)
        key = lax.broadcasted_iota(jnp.int32, s.shape, 1)
        online([jnp.where(key <= tok, s, -jnp.inf)], [kn[:, :KV_LORA]])

    vals = [r[0, 0].astype(BF16) for r in ckv_refs]
    parts = [
        lax.dot_general(q_lat, vals[i], nt, preferred_element_type=F32)
        + jnp.dot(q_rope, kr_refs[i][0, 0].astype(BF16), preferred_element_type=F32)
        for i in range(npg)
    ]
    online(parts, vals)

    @pl.when(step == pl.num_programs(1) - 1)
    def _():
        inv = 1.0 / l_ref[...]
        o_ref[0] = (acc_ref[...] * jnp.concatenate([inv] * (KV_LORA // LANES), axis=1)).astype(BF16)


def paged_attention(page_table, q_rows, k_new, ckv_pool, kr_pool_t, layer):
    npg = PAGES_PER_STEP
    rows = MLA_HEADS * DEC_SEQ

    def page_spec(shape, i):
        return pl.BlockSpec((1, 1) + shape, lambda b, p, pt: (layer, pt[b, p * npg + i], 0, 0))

    grid_spec = pltpu.PrefetchScalarGridSpec(
        num_scalar_prefetch=1,
        grid=(DEC_BATCH, N_PAGES // npg),
        in_specs=[
            pl.BlockSpec((1, rows, QK_WIDE), lambda b, p, pt: (b, 0, 0)),
            pl.BlockSpec((1, MLSTM_CHUNK_S, QK_WIDE), lambda b, p, pt: (b, 0, 0)),
        ] + [page_spec((PAGE_SIZE, KV_LORA), i) for i in range(npg)]
        + [page_spec((QK_ROPE, PAGE_SIZE), i) for i in range(npg)],
        out_specs=pl.BlockSpec((1, rows, KV_LORA), lambda b, p, pt: (b, 0, 0)),
        scratch_shapes=[
            pltpu.VMEM((rows, KV_LORA), F32),
            pltpu.VMEM((rows, LANES), F32),
            pltpu.VMEM((rows, LANES), F32),
        ],
    )
    return pl.pallas_call(
        _paged_body,
        grid_spec=grid_spec,
        out_shape=jax.ShapeDtypeStruct((DEC_BATCH, rows, KV_LORA), BF16),
        compiler_params=_cparams(("parallel", "arbitrary")),
        name="paged_attention",
    )(page_table, q_rows, k_new, *([ckv_pool] * npg), *([kr_pool_t] * npg))


def _mla_out_body(ol_ref, wv_ref, wo_ref, r_ref, out_ref, o_scr):
    for h in range(MLA_HEADS):
        oh = jnp.dot(ol_ref[:, h * KV_LORA:(h + 1) * KV_LORA], wv_ref[h], preferred_element_type=F32)
        o_scr[:, h * V_HEAD:(h + 1) * V_HEAD] = oh.astype(BF16)
    out_ref[...] = r_ref[...] + jnp.dot(o_scr[...], wo_ref[...], preferred_element_type=F32)


def mla_out(h, o_lat, wv, wo, *, row_block_offset):
    m = o_lat.shape[0]
    tm = TM_ROW
    return pl.pallas_call(
        _mla_out_body,
        grid=(m // tm,),
        in_specs=[
            pl.BlockSpec((tm, MLA_HEADS * KV_LORA), lambda i: (i, 0)),
            pl.BlockSpec((MLA_HEADS, KV_LORA, V_HEAD), lambda i: (0, 0, 0)),
            pl.BlockSpec((MLA_HEADS * V_HEAD, D_MODEL), lambda i: (0, 0)),
            pl.BlockSpec((tm, D_MODEL), lambda i: (i + row_block_offset, 0)),
        ],
        out_specs=pl.BlockSpec((tm, D_MODEL), lambda i: (i + row_block_offset, 0)),
        out_shape=jax.ShapeDtypeStruct(h.shape, F32),
        scratch_shapes=[pltpu.VMEM((tm, MLA_HEADS * V_HEAD), BF16)],
        input_output_aliases={3: 0},
        compiler_params=_cparams(("parallel",)),
        name="mla_out",
    )(o_lat, wv, wo, h)


def _router_body(h_ref, g_ref, w_ref, b_ref, u_ref, e_ref, gate_ref):
    u = _rms(h_ref[...], g_ref[...])
    u_ref[...] = u
    lg = jnp.dot(u.astype(BF16), w_ref[...], preferred_element_type=F32) + b_ref[...]
    lane = lax.broadcasted_iota(jnp.int32, lg.shape, 1)
    far = jnp.int32(LANES)
    lgm = jnp.where(lane < N_GROUPS, lg, -jnp.inf)
    mg = jnp.max(lgm, axis=-1, keepdims=True)
    pg_sel = 1.0 / jnp.sum(jnp.exp(lgm - mg), axis=-1, keepdims=True)
    g_sel = jnp.min(jnp.where(lgm == mg, lane, far), axis=-1, keepdims=True)
    lo = N_GROUPS + EXPERTS_PER_GROUP * g_sel
    in_group = jnp.logical_and(lane >= lo, lane < lo + EXPERTS_PER_GROUP)
    lem = jnp.where(in_group, lg, -jnp.inf)
    me = jnp.max(lem, axis=-1, keepdims=True)
    ee = jnp.exp(lem - me)
    pe = ee / jnp.sum(ee, axis=-1, keepdims=True)
    pe = jnp.where(in_group, pe, -1.0)
    p1 = jnp.max(pe, axis=-1, keepdims=True)
    i1 = jnp.min(jnp.where(pe == p1, lane, far), axis=-1, keepdims=True)
    pe2 = jnp.where(lane == i1, -1.0, pe)
    p2 = jnp.max(pe2, axis=-1, keepdims=True)
    i2 = jnp.min(jnp.where(pe2 == p2, lane, far), axis=-1, keepdims=True)
    norm = pg_sel / (p1 + p2)
    e_ref[...] = jnp.where(lane == 0, i1 - N_GROUPS, jnp.where(lane == 1, i2 - N_GROUPS, 0))
    gate_ref[...] = jnp.where(lane == 0, p1 * norm, jnp.where(lane == 1, p2 * norm, 0.0))


def router(h, g, w, b):
    n, d = h.shape
    tm = TM_ROW
    row = lambda i: (i, 0)
    fixed = lambda i: (0, 0)
    return pl.pallas_call(
        _router_body,
        grid=(n // tm,),
        in_specs=[
            pl.BlockSpec((tm, d), row),
            pl.BlockSpec((1, d), fixed),
            pl.BlockSpec((d, LANES), fixed),
            pl.BlockSpec((1, LANES), fixed),
        ],
        out_specs=[pl.BlockSpec((tm, d), row), pl.BlockSpec((tm, LANES), row), pl.BlockSpec((tm, LANES), row)],
        out_shape=[
            jax.ShapeDtypeStruct((n, d), F32),
            jax.ShapeDtypeStruct((n, LANES), jnp.int32),
            jax.ShapeDtypeStruct((n, LANES), F32),
        ],
        compiler_params=_cparams(("parallel",)),
        name="router",
    )(h, g, w, b)


def _gather_body(idx_ref, src_ref, out_ref, sem):
    base = pl.program_id(0) * GATHER_ROWS

    def row_copy(r, src_row):
        return pltpu.make_async_copy(src_ref.at[pl.ds(src_row, 1)], out_ref.at[pl.ds(r, 1)], sem)

    def start(r, carry):
        row_copy(r, idx_ref[base + r]).start()
        return carry

    def wait(r, carry):
        row_copy(r, 0).wait()
        return carry

    lax.fori_loop(0, GATHER_ROWS, start, 0)
    lax.fori_loop(0, GATHER_ROWS, wait, 0)


def gather_rows(src, idx):
    n_out = idx.shape[0]
    width = src.shape[1]
    grid_spec = pltpu.PrefetchScalarGridSpec(
        num_scalar_prefetch=1,
        grid=(n_out // GATHER_ROWS,),
        in_specs=[pl.BlockSpec(memory_space=pl.ANY)],
        out_specs=pl.BlockSpec((GATHER_ROWS, width), lambda i, idx: (i, 0)),
        scratch_shapes=[pltpu.SemaphoreType.DMA(())],
    )
    return pl.pallas_call(
        _gather_body,
        grid_spec=grid_spec,
        out_shape=jax.ShapeDtypeStruct((n_out, width), src.dtype),
        compiler_params=_cparams(("arbitrary",)),
        name="gather_rows",
    )(idx, src)


def _experts_body(be_ref, nu_ref, x_ref, w1_ref, w3_ref, w2_ref, y_ref, w1b, w3b, w2b):
    b = pl.program_id(0)
    changed = jnp.logical_or(b == 0, be_ref[b] != be_ref[jnp.maximum(b - 1, 0)])

    @pl.when(changed)
    def _():
        w1b[...] = w1_ref[0, 0].astype(BF16)
        w3b[...] = w3_ref[0, 0].astype(BF16)
        w2b[...] = w2_ref[0, 0].astype(BF16)

    @pl.when(b < nu_ref[0])
    def _():
        x = x_ref[...].astype(BF16)
        a = jnp.dot(x, w1b[...], preferred_element_type=F32)
        g = jnp.dot(x, w3b[...], preferred_element_type=F32)
        mid = (a * _sigmoid(a) * g).astype(BF16)
        y_ref[...] = jnp.dot(mid, w2b[...], preferred_element_type=F32)

    @pl.when(b >= nu_ref[0])
    def _():
        y_ref[...] = jnp.zeros(y_ref.shape, F32)


def experts_ffn(block_expert, n_used, x_pad, w1, w3, w2, layer):
    n_slots, d = x_pad.shape
    n_blocks = n_slots // MOE_BM
    grid_spec = pltpu.PrefetchScalarGridSpec(
        num_scalar_prefetch=2,
        grid=(n_blocks,),
        in_specs=[
            pl.BlockSpec((MOE_BM, d), lambda b, be, nu: (b, 0)),
            pl.BlockSpec((1, 1, d, D_EXPERT), lambda b, be, nu: (layer, be[b], 0, 0)),
            pl.BlockSpec((1, 1, d, D_EXPERT), lambda b, be, nu: (layer, be[b], 0, 0)),
            pl.BlockSpec((1, 1, D_EXPERT, d), lambda b, be, nu: (layer, be[b], 0, 0)),
        ],
        out_specs=pl.BlockSpec((MOE_BM, d), lambda b, be, nu: (b, 0)),
        scratch_shapes=[
            pltpu.VMEM((d, D_EXPERT), BF16),
            pltpu.VMEM((d, D_EXPERT), BF16),
            pltpu.VMEM((D_EXPERT, d), BF16),
        ],
    )
    return pl.pallas_call(
        _experts_body,
        grid_spec=grid_spec,
        out_shape=jax.ShapeDtypeStruct((n_slots, d), F32),
        compiler_params=_cparams(("arbitrary",)),
        name="experts_ffn",
    )(block_expert, n_used, x_pad, w1, w3, w2)


def moe_dispatch(experts):
    a = N_TOK * TOP_K
    n_blocks = a // MOE_BM + N_EXPERTS
    flat_e = experts.reshape(a)
    order = jnp.argsort(flat_e)
    sorted_e = flat_e[order]
    counts = jnp.bincount(flat_e, length=N_EXPERTS)
    padded = (counts + MOE_BM - 1) // MOE_BM * MOE_BM
    start = jnp.cumsum(counts) - counts
    pad_end = jnp.cumsum(padded)
    pad_start = pad_end - padded
    slot_sorted = (pad_start[sorted_e] + jnp.arange(a) - start[sorted_e]).astype(jnp.int32)
    slot = jnp.zeros((a,), jnp.int32).at[order].set(slot_sorted)
    n_slots = n_blocks * MOE_BM
    token_of_slot = (jnp.arange(n_slots, dtype=jnp.int32) % N_TOK).at[slot].set(
        jnp.arange(a, dtype=jnp.int32) // TOP_K)
    block_expert = jnp.minimum(
        jnp.searchsorted(pad_end, jnp.arange(n_blocks) * MOE_BM, side='right'), N_EXPERTS - 1).astype(jnp.int32)
    n_used = (pad_end[-1:] // MOE_BM).astype(jnp.int32)
    return slot, token_of_slot, block_expert, n_used


def _ple_body(h_ref, y0_ref, y1_ref, gate_ref, g_ref, wg_ref, p_ref, wp_ref, gf_ref, out_ref, *, final):
    gt = _round_bf16(gate_ref[...])
    h2 = h_ref[...] + (gt[:, 0:1] * _round_bf16(y0_ref[...]) + gt[:, 1:2] * _round_bf16(y1_ref[...]))
    xn = _rms(h2, g_ref[...]).astype(BF16)
    gate = _sigmoid(jnp.dot(xn, wg_ref[...], preferred_element_type=F32))
    emb = jnp.dot(p_ref[...].astype(BF16), wp_ref[...], preferred_element_type=F32)
    h3 = h2 + gate * emb
    out_ref[...] = _rms(h3, gf_ref[...]) if final else h3


def ple(h, y_kmajor, gates, g, wg, p, wp, g_final, *, final):
    n, d = h.shape
    tm = TM_ROW
    row = lambda i: (i, 0)
    fixed = lambda i: (0, 0)
    return pl.pallas_call(
        functools.partial(_ple_body, final=final),
        grid=(n // tm,),
        in_specs=[
            pl.BlockSpec((tm, d), row),
            pl.BlockSpec((tm, d), row),
            pl.BlockSpec((tm, d), lambda i: (i + n // tm, 0)),
            pl.BlockSpec((tm, LANES), row),
            pl.BlockSpec((1, d), fixed),
            pl.BlockSpec((d, d), fixed),
            pl.BlockSpec((tm, PLE_DIM), row),
            pl.BlockSpec((PLE_DIM, d), fixed),
            pl.BlockSpec((1, d), fixed),
        ],
        out_specs=pl.BlockSpec((tm, d), row),
        out_shape=jax.ShapeDtypeStruct((n, d), F32),
        compiler_params=_cparams(("parallel",)),
        name="ple",
    )(h, y_kmajor, y_kmajor, gates, g, wg, p, wp, g_final)


def _pad_cols(w, width):
    return jnp.pad(w, ((0, 0), (0, width - w.shape[1])))


def _rope_tables():
    pos = jnp.concatenate([jnp.tile(jnp.arange(SEQ), BATCH), jnp.tile(PAST_LEN + jnp.arange(DEC_SEQ), DEC_BATCH)])
    inv_freq = ROPE_THETA ** (-jnp.arange(0, QK_ROPE, 2, dtype=F32) / QK_ROPE)
    ang = pos.astype(F32)[:, None] * inv_freq[None, :]
    cos, sin = jnp.cos(ang), jnp.sin(ang)
    cc = jnp.concatenate([cos, cos], axis=1)
    ss = jnp.concatenate([-sin, sin], axis=1)
    return jnp.tile(cc, (1, LANES // QK_ROPE)), jnp.tile(ss, (1, LANES // QK_ROPE))


def _swap_halves(w):
    half = w.shape[-1] // 2
    return jnp.concatenate([w[..., half:], w[..., :half]], axis=-1)


def _mlstm_layer(h, j, ln, w_in, b_if, g_head, w_out, state_c, state_n, state_m, c_s_other):
    w_main = w_in[:, :MLSTM_MAIN].astype(BF16)
    w_gate = _pad_cols(w_in[:, MLSTM_MAIN:], LANES).astype(BF16)
    qkv, o_gate, gates = norm_proj(h, ln.reshape(1, D_MODEL), w_main, w_gate)
    bias = _pad_cols(b_if.reshape(1, 2 * MLSTM_HEADS), LANES)
    gh = g_head.reshape(1, D_MODEL)

    zc = jnp.zeros((1, BATCH, MLSTM_HEADS, MLSTM_DK, MLSTM_DV), F32)
    zn = jnp.zeros((BATCH, MLSTM_HEADS, MLSTM_DK), F32)
    zm = jnp.zeros((BATCH, 1, LANES), F32)
    gated_p, c_p, n_p, m_p = mlstm_scan(qkv, o_gate, gates, bias, gh, zc, zn, zm, n_seq=BATCH,
                                        n_chunks=SEQ // MLSTM_CHUNK_P, chunk=MLSTM_CHUNK_P, valid=MLSTM_CHUNK_P)

    def pad_rows(a):
        a = a[N_PROMPT:].reshape(DEC_BATCH, DEC_SEQ, a.shape[1])
        return jnp.pad(a, ((0, 0), (0, MLSTM_CHUNK_S - DEC_SEQ), (0, 0))).reshape(-1, a.shape[2])

    m0 = _pad_cols(state_m[j], LANES).reshape(DEC_BATCH, 1, LANES)
    gated_s, c_s, n_s, m_s = mlstm_scan(pad_rows(qkv), pad_rows(o_gate), pad_rows(gates), bias, gh, state_c,
                                        state_n[j], m0, n_seq=DEC_BATCH, n_chunks=1, chunk=MLSTM_CHUNK_S,
                                        valid=DEC_SEQ, layer=j, c_other=c_s_other)
    gated_s = gated_s.reshape(DEC_BATCH, MLSTM_CHUNK_S, D_MODEL)[:, :DEC_SEQ].reshape(N_SAMPLE, D_MODEL)

    w_o = w_out.astype(BF16)
    h = linear_res(h, gated_p, w_o, row_block_offset=0)
    h = linear_res(h, gated_s, w_o, row_block_offset=N_PROMPT // TM)
    state_p = (c_p[0], n_p, m_p[:, 0, :MLSTM_HEADS])
    state_s = (n_s, m_s[:, 0, :MLSTM_HEADS])
    return h, state_p, state_s, c_s


def _mla_layer(h, j, ln, w_a, g_q, w_qb, g_kv, w_kvb, w_o, cc, ss, cache_ckv, cache_kr, page_table):
    w_kr = w_a[:, Q_LORA + KV_LORA:]
    w_a_ext = jnp.concatenate([w_a, _swap_halves(w_kr)], axis=1).astype(BF16)
    cq, ckv, kr, k_wide = mla_a(h, ln.reshape(1, D_MODEL), w_a_ext, g_q.reshape(1, Q_LORA),
                                g_kv.reshape(1, KV_LORA), cc, ss)

    wq = w_qb.reshape(Q_LORA, MLA_HEADS, QK_NOPE + QK_ROPE)
    w_nope = wq[:, :, :QK_NOPE].reshape(Q_LORA, MLA_HEADS * QK_NOPE).astype(BF16)
    w_rope = wq[:, :, QK_NOPE:]
    pad = ((0, 0), (0, 0), (0, LANES - QK_ROPE))
    w_r = jnp.pad(w_rope, pad).reshape(Q_LORA, MLA_HEADS * LANES).astype(BF16)
    w_rs = jnp.pad(_swap_halves(w_rope), pad).reshape(Q_LORA, MLA_HEADS * LANES).astype(BF16)
    wkv = w_kvb.reshape(KV_LORA, MLA_HEADS, QK_NOPE + V_HEAD)
    wk_t = jnp.transpose(wkv[:, :, :QK_NOPE], (1, 2, 0)).astype(BF16)
    wv = jnp.transpose(wkv[:, :, QK_NOPE:], (1, 0, 2)).astype(BF16)
    q_wide = mla_q(cq, w_nope, w_r, w_rs, wk_t, cc, ss)

    o_lat_p = flash_attention(q_wide, k_wide)

    q_s = q_wide[N_PROMPT:].reshape(DEC_BATCH, DEC_SEQ, MLA_HEADS, QK_WIDE)
    q_s = jnp.transpose(q_s, (0, 2, 1, 3)).reshape(DEC_BATCH, MLA_HEADS * DEC_SEQ, QK_WIDE)
    k_new = jnp.pad(k_wide[N_PROMPT:].reshape(DEC_BATCH, DEC_SEQ, QK_WIDE),
                    ((0, 0), (0, MLSTM_CHUNK_S - DEC_SEQ), (0, 0)))
    o_s = paged_attention(page_table, q_s, k_new, cache_ckv, jnp.swapaxes(cache_kr, 2, 3), j)
    o_lat_s = jnp.transpose(o_s.reshape(DEC_BATCH, MLA_HEADS, DEC_SEQ, KV_LORA), (0, 2, 1, 3))
    o_lat_s = o_lat_s.reshape(N_SAMPLE, MLA_HEADS * KV_LORA)

    wo = w_o.astype(BF16)
    h = mla_out(h, o_lat_p, wv, wo, row_block_offset=0)
    h = mla_out(h, o_lat_s, wv, wo, row_block_offset=N_PROMPT // TM_ROW)
    return h, ckv, kr


def _moe_ple_layer(h, i, ln_ffn, w_rg, b_rg, w_re, b_re, w1, w3, w2, p_all, ple_g, ple_wg, ple_wp, ln_final):
    w_r = _pad_cols(jnp.concatenate([w_rg, w_re], axis=1), LANES).astype(BF16)
    b_r = _pad_cols(jnp.concatenate([b_rg, b_re]).reshape(1, -1), LANES)
    u, e_idx, gates = router(h, ln_ffn.reshape(1, D_MODEL), w_r, b_r)
    slot, token_of_slot, block_expert, n_used = moe_dispatch(e_idx[:, :TOP_K])
    x_pad = gather_rows(u, token_of_slot)
    y_pad = experts_ffn(block_expert, n_used, x_pad, w1, w3, w2, i)
    y_kmajor = gather_rows(y_pad, slot.reshape(N_TOK, TOP_K).T.reshape(-1))
    return ple(h, y_kmajor, gates, ple_g.reshape(1, D_MODEL), ple_wg.astype(BF16), p_all, ple_wp.astype(BF16),
               ln_final.reshape(1, D_MODEL), final=(i == DEPTH - 1))


def kernel(x_prompt, x_sample, cache_mla_ckv, cache_mla_krope, state_mlstm_C, state_mlstm_n, state_mlstm_m,
           page_table, p_prompt, p_sample, ln_mix, ln_ffn, ln_final, mlstm_w_in, mlstm_b_if, mlstm_g_head,
           mlstm_w_out, mla_w_a, mla_g_q, mla_w_qb, mla_g_kv, mla_w_kvb, mla_w_o, moe_w_rg, moe_b_rg, moe_w_re,
           moe_b_re, moe_w1, moe_w3, moe_w2, ple_g, ple_w_gate, ple_w_proj):
    h = jnp.concatenate([x_prompt.reshape(N_PROMPT, D_MODEL), x_sample.reshape(N_SAMPLE, D_MODEL)], axis=0)
    cc, ss = _rope_tables()
    ckv_rows, kr_rows, states_p, states_s = [], [], [], []
    c_s = None
    for i in range(DEPTH):
        j = i // 2
        if i % 2 == 0:
            h, st_p, st_s, c_s = _mlstm_layer(h, j, ln_mix[i], mlstm_w_in[j], mlstm_b_if[j], mlstm_g_head[j],
                                              mlstm_w_out[j], state_mlstm_C, state_mlstm_n, state_mlstm_m, c_s)
            states_p.append(st_p)
            states_s.append(st_s)
        else:
            h, ckv, kr = _mla_layer(h, j, ln_mix[i], mla_w_a[j], mla_g_q[j], mla_w_qb[j], mla_g_kv[j],
                                    mla_w_kvb[j], mla_w_o[j], cc, ss, cache_mla_ckv, cache_mla_krope, page_table)
            ckv_rows.append(ckv)
            kr_rows.append(kr)
        p_all = jnp.concatenate([p_prompt[i].reshape(N_PROMPT, PLE_DIM), p_sample[i].reshape(N_SAMPLE, PLE_DIM)])
        h = _moe_ple_layer(h, i, ln_ffn[i], moe_w_rg[i], moe_b_rg[i], moe_w_re[i], moe_b_re[i], moe_w1, moe_w3,
                           moe_w2, p_all, ple_g[i], ple_w_gate[i], ple_w_proj[i], ln_final)

    ckv_all = jnp.stack(ckv_rows)
    kr_all = jnp.stack(kr_rows)

    def split(a, width):
        return (a[:, :N_PROMPT].reshape(-1, BATCH, SEQ, width), a[:, N_PROMPT:].reshape(-1, DEC_BATCH, DEC_SEQ, width))

    ckv_p, ckv_s = split(ckv_all, KV_LORA)
    kr_p, kr_s = split(kr_all, QK_ROPE)
    c_p, n_p, m_p = (jnp.stack(t) for t in zip(*states_p))
    n_s, m_s = (jnp.stack(t) for t in zip(*states_s))
    y_prompt = h[:N_PROMPT].reshape(BATCH, SEQ, D_MODEL)
    y_sample = h[N_PROMPT:].reshape(DEC_BATCH, DEC_SEQ, D_MODEL)
    return (y_prompt, y_sample, ckv_p, kr_p, c_p, n_p, m_p, ckv_s, kr_s, c_s, n_s, m_s)
```

```python
import functools

import jax
import jax.numpy as jnp
from jax import lax
from jax.experimental import pallas as pl
from jax.experimental.pallas import tpu as pltpu

F32 = jnp.float32
BF16 = jnp.bfloat16
HIGHEST = lax.Precision.HIGHEST

D_MODEL = 2048
BATCH = 2
SEQ = 4096
DEPTH = 4
DEC_BATCH = 128
DEC_SEQ = 4
PAST_LEN = 8192
PAGE_SIZE = 128
N_PAGES = PAST_LEN // PAGE_SIZE
EPS = 1e-6

MLSTM_HEADS = 8
MLSTM_DV = D_MODEL // MLSTM_HEADS
MLSTM_DK = MLSTM_DV // 2
MLSTM_QK = MLSTM_HEADS * MLSTM_DK
GATE_SOFTCAP = 15.0
MLSTM_MAIN = 2 * MLSTM_QK + 2 * D_MODEL
QKV_TILES = (2 * MLSTM_QK + D_MODEL) // MLSTM_QK

MLA_HEADS = 16
QK_NOPE = 128
QK_ROPE = 64
V_HEAD = 128
Q_LORA = 512
KV_LORA = 512
ROPE_THETA = 10000.0
ATTN_SCALE = (QK_NOPE + QK_ROPE) ** -0.5
QK_WIDE = KV_LORA + 128

N_GROUPS = 8
EXPERTS_PER_GROUP = 8
N_EXPERTS = N_GROUPS * EXPERTS_PER_GROUP
TOP_K = 2
D_EXPERT = 512
PLE_DIM = 256

N_PROMPT = BATCH * SEQ
N_SAMPLE = DEC_BATCH * DEC_SEQ
N_TOK = N_PROMPT + N_SAMPLE

LANES = 128
NEG = -1e30
VMEM_LIMIT = 56 * 1024 * 1024

TM = 512
TM_ROW = 256
MLSTM_CHUNK_P = 64
MLSTM_CHUNK_S = 16
TQ = 256
TK = 1024
Q_PER_K = TK // TQ
FLASH_ROWS = 512
EXP2_SCALE = ATTN_SCALE * 1.4426950408889634
PAGES_PER_STEP = 32
Q_HEADS_PER_STEP = 4
MOE_BM = 256
GATHER_ROWS = 256


def _cparams(sem):
    return pltpu.CompilerParams(dimension_semantics=sem, vmem_limit_bytes=VMEM_LIMIT)


def _rms(x, g):
    return x * lax.rsqrt(jnp.mean(x * x, axis=-1, keepdims=True) + EPS) * g


def _sigmoid(x):
    return 1.0 / (1.0 + jnp.exp(-x))


def _round_bf16(x):
    return x.astype(BF16).astype(F32)


def _norm_proj_body(h_ref, g_ref, w_ref, wg_ref, qkv_ref, o_ref, gate_ref, xn_ref):
    j = pl.program_id(1)

    @pl.when(j == 0)
    def _():
        xn = _rms(h_ref[...], g_ref[...]).astype(BF16)
        xn_ref[...] = xn
        gate_ref[...] = jnp.dot(xn, wg_ref[...], preferred_element_type=F32)

    acc = jnp.dot(xn_ref[...], w_ref[...], preferred_element_type=F32)

    @pl.when(j == 0)
    def _():
        qkv_ref[...] = (acc * MLSTM_DK ** -0.5).astype(BF16)

    @pl.when(jnp.logical_and(j > 0, j < QKV_TILES))
    def _():
        qkv_ref[...] = acc.astype(BF16)

    @pl.when(j >= QKV_TILES)
    def _():
        o_ref[...] = acc


def norm_proj(h, g, w, wg):
    n, d = h.shape
    tn = MLSTM_QK
    return pl.pallas_call(
        _norm_proj_body,
        grid=(n // TM, MLSTM_MAIN // tn),
        in_specs=[
            pl.BlockSpec((TM, d), lambda i, j: (i, 0)),
            pl.BlockSpec((1, d), lambda i, j: (0, 0)),
            pl.BlockSpec((d, tn), lambda i, j: (0, j)),
            pl.BlockSpec((d, LANES), lambda i, j: (0, 0)),
        ],
        out_specs=[
            pl.BlockSpec((TM, tn), lambda i, j: (i, jnp.minimum(j, QKV_TILES - 1))),
            pl.BlockSpec((TM, tn), lambda i, j: (i, jnp.maximum(j - QKV_TILES, 0))),
            pl.BlockSpec((TM, LANES), lambda i, j: (i, 0)),
        ],
        out_shape=[
            jax.ShapeDtypeStruct((n, 2 * MLSTM_QK + D_MODEL), BF16),
            jax.ShapeDtypeStruct((n, D_MODEL), F32),
            jax.ShapeDtypeStruct((n, LANES), F32),
        ],
        scratch_shapes=[pltpu.VMEM((TM, d), BF16)],
        compiler_params=_cparams(("parallel", "arbitrary")),
        name="norm_proj",
    )(h, g, w, wg)


def _linear_res_body(x_ref, w_ref, r_ref, o_ref):
    o_ref[...] = r_ref[...] + jnp.dot(x_ref[...], w_ref[...], preferred_element_type=F32)


def linear_res(h, x, w, *, row_block_offset, tn=1024):
    m, k = x.shape
    d = w.shape[1]
    return pl.pallas_call(
        _linear_res_body,
        grid=(m // TM, d // tn),
        in_specs=[
            pl.BlockSpec((TM, k), lambda i, j: (i, 0)),
            pl.BlockSpec((k, tn), lambda i, j: (0, j)),
            pl.BlockSpec((TM, tn), lambda i, j: (i + row_block_offset, j)),
        ],
        out_specs=pl.BlockSpec((TM, tn), lambda i, j: (i + row_block_offset, j)),
        out_shape=jax.ShapeDtypeStruct(h.shape, F32),
        input_output_aliases={2: 0},
        compiler_params=_cparams(("parallel", "parallel")),
        name="linear_res",
    )(x, w, h)


def _mlstm_body(q_ref, k_ref, v_ref, o_ref, gt_ref, bif_ref, gh_ref, c0_ref, n0_ref, m0_ref, *rest, chunk, valid,
                layer):
    out_ref, c_ref, n_ref, m_ref = rest[-4:]
    L = chunk

    @pl.when(pl.program_id(1) == 0)
    def _():
        c_ref[layer] = c0_ref[0]
        for other in range(c_ref.shape[0]):
            if other != layer:
                c_ref[other] = rest[0][0] if len(rest) > 4 else jnp.zeros(c_ref.shape[1:], F32)
        n_ref[...] = n0_ref[...]
        m_ref[...] = m0_ref[...]

    gc = GATE_SOFTCAP * jnp.tanh((gt_ref[...] + bif_ref[...]) / GATE_SOFTCAP)
    logf_all = jnp.minimum(gc, 0.0) - jnp.log1p(jnp.exp(-jnp.abs(gc)))
    logi_all = gc
    if valid < L:
        live = lax.broadcasted_iota(jnp.int32, (L, 1), 0) < valid
        logi_all = jnp.where(live, gc, NEG)
        logf_all = jnp.where(live, logf_all, 0.0)
    rows = lax.broadcasted_iota(jnp.int32, (L, L), 0)
    cols = lax.broadcasted_iota(jnp.int32, (L, L), 1)
    causal = rows >= cols
    eye = rows == cols
    b_all = jnp.dot(causal.astype(F32), logf_all, precision=HIGHEST, preferred_element_type=F32)

    def as_row(col):
        return jnp.sum(jnp.where(eye, col, 0.0), axis=0, keepdims=True)

    for hd in range(MLSTM_HEADS):
        b_col = b_all[:, MLSTM_HEADS + hd:MLSTM_HEADS + hd + 1]
        li_col = logi_all[:, hd:hd + 1]
        b_row = as_row(b_col)
        li_row = as_row(li_col)
        m0 = m_ref[0, :, hd:hd + 1]
        d = jnp.where(causal, b_col - b_row + li_row, NEG)
        carry = b_col + m0
        m_row = jnp.maximum(carry, jnp.max(d, axis=-1, keepdims=True))
        q = q_ref[:, hd * MLSTM_DK:(hd + 1) * MLSTM_DK]
        k = k_ref[:, hd * MLSTM_DK:(hd + 1) * MLSTM_DK]
        v = v_ref[:, hd * MLSTM_DV:(hd + 1) * MLSTM_DV]
        qk = lax.dot_general(q, k, (((1,), (1,)), ((), ())), preferred_element_type=F32)
        s = qk * jnp.exp(d - m_row)
        w_prev = jnp.exp(carry - m_row)
        c_old = c_ref[layer, 0, hd]
        n_old = n_ref[0, hd:hd + 1, :]
        num = (jnp.dot(s.astype(BF16), v, preferred_element_type=F32)
               + w_prev * jnp.dot(q, c_old.astype(BF16), preferred_element_type=F32))
        den = (jnp.sum(s, axis=-1, keepdims=True)
               + w_prev * jnp.sum(q.astype(F32) * _round_bf16(n_old), axis=-1, keepdims=True))
        hh = num / jnp.maximum(jnp.abs(den), jnp.exp(-m_row))

        b_end = b_col[L - 1:L, :]
        dend = b_end - b_col + li_col
        m_new = jnp.maximum(b_end + m0, jnp.max(dend, axis=0, keepdims=True))
        a_col = jnp.exp(dend - m_new)
        keep = jnp.exp(b_end + m0 - m_new)
        kf = k.astype(F32)
        upd = lax.dot_general((a_col * kf).astype(BF16), v, (((0,), (0,)), ((), ())), preferred_element_type=F32)
        c_ref[layer, 0, hd] = keep * c_old + upd
        n_ref[0, hd:hd + 1, :] = keep * n_old + jnp.sum(_round_bf16(a_col) * kf, axis=0, keepdims=True)
        m_ref[0, :, hd:hd + 1] = m_new

        sl = slice(hd * MLSTM_DV, (hd + 1) * MLSTM_DV)
        hn = hh * lax.rsqrt(jnp.mean(hh * hh, axis=-1, keepdims=True) + EPS) * gh_ref[:, sl]
        out_ref[:, sl] = (_sigmoid(o_ref[:, sl]) * hn).astype(BF16)


def mlstm_scan(qkv, o_gate, gates, b_if, g_head, c0, n0, m0, *, n_seq, n_chunks, chunk, valid, layer=0,
               c_other=None):
    n_layers = c0.shape[0]
    assert n_layers <= 2
    rows = n_seq * n_chunks * chunk
    row = lambda s, c: s * n_chunks + c
    body = functools.partial(_mlstm_body, chunk=chunk, valid=valid, layer=layer)
    state_shape = (MLSTM_HEADS, MLSTM_DK, MLSTM_DV)
    c_in_block = pl.BlockSpec((1, 1) + state_shape, lambda s, c: (layer, s, 0, 0, 0))
    c_out_block = pl.BlockSpec((n_layers, 1) + state_shape, lambda s, c: (0, s, 0, 0, 0))
    extra_specs, extra_args = [], []
    if c_other is not None:
        extra_specs = [pl.BlockSpec((1, 1) + state_shape, lambda s, c: (1 - layer, s, 0, 0, 0))]
        extra_args = [c_other]
    return pl.pallas_call(
        body,
        grid=(n_seq, n_chunks),
        in_specs=[
            pl.BlockSpec((chunk, MLSTM_QK), lambda s, c: (row(s, c), 0)),
            pl.BlockSpec((chunk, MLSTM_QK), lambda s, c: (row(s, c), 1)),
            pl.BlockSpec((chunk, D_MODEL), lambda s, c: (row(s, c), 1)),
            pl.BlockSpec((chunk, D_MODEL), lambda s, c: (row(s, c), 0)),
            pl.BlockSpec((chunk, LANES), lambda s, c: (row(s, c), 0)),
            pl.BlockSpec((1, LANES), lambda s, c: (0, 0)),
            pl.BlockSpec((1, D_MODEL), lambda s, c: (0, 0)),
            c_in_block,
            pl.BlockSpec((1, MLSTM_HEADS, MLSTM_DK), lambda s, c: (s, 0, 0)),
            pl.BlockSpec((1, 1, LANES), lambda s, c: (s, 0, 0)),
        ] + extra_specs,
        out_specs=[
            pl.BlockSpec((chunk, D_MODEL), lambda s, c: (row(s, c), 0)),
            c_out_block,
            pl.BlockSpec((1, MLSTM_HEADS, MLSTM_DK), lambda s, c: (s, 0, 0)),
            pl.BlockSpec((1, 1, LANES), lambda s, c: (s, 0, 0)),
        ],
        out_shape=[
            jax.ShapeDtypeStruct((rows, D_MODEL), BF16),
            jax.ShapeDtypeStruct(c0.shape, F32),
            jax.ShapeDtypeStruct((n_seq, MLSTM_HEADS, MLSTM_DK), F32),
            jax.ShapeDtypeStruct((n_seq, 1, LANES), F32),
        ],
        compiler_params=_cparams(("parallel", "arbitrary")),
        name="mlstm_scan",
    )(qkv, qkv, qkv, o_gate, gates, b_if, g_head, c0, n0, m0, *extra_args)


def _mla_a_body(h_ref, g_ref, w_ref, gq_ref, gkv_ref, cc_ref, ss_ref, cq_ref, ckv_ref, kr_ref, kw_ref):
    xn = _rms(h_ref[...], g_ref[...]).astype(BF16)
    a = jnp.dot(xn, w_ref[...], preferred_element_type=F32)
    cq_ref[...] = _rms(a[:, :Q_LORA], gq_ref[...]).astype(BF16)
    ckv = _rms(a[:, Q_LORA:Q_LORA + KV_LORA], gkv_ref[...])
    ckv_ref[...] = ckv
    t = a[:, Q_LORA + KV_LORA:]
    kr = t[:, :QK_ROPE] * cc_ref[:, :QK_ROPE] + t[:, QK_ROPE:] * ss_ref[:, :QK_ROPE]
    kr_ref[...] = kr
    kw_ref[:, :KV_LORA] = ckv.astype(BF16)
    kw_ref[:, KV_LORA:] = jnp.concatenate([kr, jnp.zeros_like(kr)], axis=1).astype(BF16)


def mla_a(h, g, w, gq, gkv, cc, ss):
    n, d = h.shape
    wa = w.shape[1]
    tm = TM_ROW
    row = lambda i: (i, 0)
    fixed = lambda i: (0, 0)
    return pl.pallas_call(
        _mla_a_body,
        grid=(n // tm,),
        in_specs=[
            pl.BlockSpec((tm, d), row),
            pl.BlockSpec((1, d), fixed),
            pl.BlockSpec((d, wa), fixed),
            pl.BlockSpec((1, Q_LORA), fixed),
            pl.BlockSpec((1, KV_LORA), fixed),
            pl.BlockSpec((tm, LANES), row),
            pl.BlockSpec((tm, LANES), row),
        ],
        out_specs=[
            pl.BlockSpec((tm, Q_LORA), row),
            pl.BlockSpec((tm, KV_LORA), row),
            pl.BlockSpec((tm, QK_ROPE), row),
            pl.BlockSpec((tm, QK_WIDE), row),
        ],
        out_shape=[
            jax.ShapeDtypeStruct((n, Q_LORA), BF16),
            jax.ShapeDtypeStruct((n, KV_LORA), F32),
            jax.ShapeDtypeStruct((n, QK_ROPE), F32),
            jax.ShapeDtypeStruct((n, QK_WIDE), BF16),
        ],
        compiler_params=_cparams(("parallel",)),
        name="mla_a",
    )(h, g, w, gq, gkv, cc, ss)


def _mla_q_body(c_ref, wn_ref, wr_ref, wrs_ref, wk_ref, cc_ref, ss_ref, out_ref):
    c = c_ref[...]
    qn = jnp.dot(c, wn_ref[...], preferred_element_type=F32).astype(BF16)
    r1 = jnp.dot(c, wr_ref[...], preferred_element_type=F32)
    r2 = jnp.dot(c, wrs_ref[...], preferred_element_type=F32)
    cc = cc_ref[...]
    ss = ss_ref[...]
    for h in range(Q_HEADS_PER_STEP):
        hs = slice(h * LANES, (h + 1) * LANES)
        qlat = jnp.dot(qn[:, hs], wk_ref[h], preferred_element_type=F32)
        out_ref[:, h * QK_WIDE:h * QK_WIDE + KV_LORA] = qlat.astype(BF16)
        out_ref[:, h * QK_WIDE + KV_LORA:(h + 1) * QK_WIDE] = (r1[:, hs] * cc + r2[:, hs] * ss).astype(BF16)


def mla_q(cq, wn, wr, wrs, wk_t, cc, ss):
    n = cq.shape[0]
    hp = Q_HEADS_PER_STEP
    head_cols = lambda i, h: (0, h)
    return pl.pallas_call(
        _mla_q_body,
        grid=(n // TM, MLA_HEADS // hp),
        in_specs=[
            pl.BlockSpec((TM, Q_LORA), lambda i, h: (i, 0)),
            pl.BlockSpec((Q_LORA, hp * QK_NOPE), head_cols),
            pl.BlockSpec((Q_LORA, hp * LANES), head_cols),
            pl.BlockSpec((Q_LORA, hp * LANES), head_cols),
            pl.BlockSpec((hp, QK_NOPE, KV_LORA), lambda i, h: (h, 0, 0)),
            pl.BlockSpec((TM, LANES), lambda i, h: (i, 0)),
            pl.BlockSpec((TM, LANES), lambda i, h: (i, 0)),
        ],
        out_specs=pl.BlockSpec((TM, hp * QK_WIDE), lambda i, h: (i, h)),
        out_shape=jax.ShapeDtypeStruct((n, MLA_HEADS * QK_WIDE), BF16),
        compiler_params=_cparams(("parallel", "arbitrary")),
        name="mla_q",
    )(cq, wn, wr, wrs, wk_t, cc, ss)


def _flash_body(q_ref, k_ref, o_ref, qs_ref, acc_ref, m_ref, l_ref):
    qi = pl.program_id(1)
    ki = pl.program_id(2)
    rows_all = MLA_HEADS * TQ
    kd = qi // Q_PER_K
    n_tiles = TK // LANES
    n_acc = KV_LORA // LANES

    @pl.when(ki == 0)
    def _():
        for h in range(MLA_HEADS):
            qs_ref[h * TQ:(h + 1) * TQ, :] = q_ref[:, h * QK_WIDE:(h + 1) * QK_WIDE]
        m_ref[...] = jnp.full(m_ref.shape, -jnp.inf, F32)
        l_ref[...] = jnp.zeros(l_ref.shape, F32)
        acc_ref[...] = jnp.zeros(acc_ref.shape, F32)

    def update(masked):
        kb = k_ref[...]
        vb = kb[:, :KV_LORA]
        for r0 in range(0, rows_all, FLASH_ROWS):
            rs = slice(r0, r0 + FLASH_ROWS)
            s = lax.dot_general(qs_ref[rs, :], kb, (((1,), (1,)), ((), ())), preferred_element_type=F32)
            if masked:
                tok = (lax.broadcasted_iota(jnp.int32, s.shape, 0) & (TQ - 1)) + (qi - kd * Q_PER_K) * TQ
                key = lax.broadcasted_iota(jnp.int32, s.shape, 1)
                s = jnp.where(key <= tok, s, -jnp.inf)
            tiles = [s[:, t * LANES:(t + 1) * LANES] for t in range(n_tiles)]
            tile_max = functools.reduce(jnp.maximum, tiles)
            m_prev = m_ref[rs, :]
            m_new = jnp.maximum(m_prev, jnp.max(tile_max, axis=-1, keepdims=True))
            ps = [jnp.exp2((t - m_new) * EXP2_SCALE) for t in tiles]
            alpha = jnp.exp2((m_prev - m_new) * EXP2_SCALE)
            row_sum = jnp.sum(functools.reduce(jnp.add, ps), axis=-1, keepdims=True)
            l_ref[rs, :] = alpha * l_ref[rs, :] + row_sum
            p = jnp.concatenate([t.astype(BF16) for t in ps], axis=1)
            pv = jnp.dot(p, vb, preferred_element_type=F32)
            for c in range(n_acc):
                cs = slice(c * LANES, (c + 1) * LANES)
                acc_ref[rs, cs] = alpha * acc_ref[rs, cs] + pv[:, cs]
            m_ref[rs, :] = m_new

    @pl.when(ki < kd)
    def _():
        update(False)

    @pl.when(ki == kd)
    def _():
        update(True)
        for h in range(MLA_HEADS):
            hs = slice(h * TQ, (h + 1) * TQ)
            inv = 1.0 / l_ref[hs, :]
            for c in range(n_acc):
                cs = slice(c * LANES, (c + 1) * LANES)
                o_ref[:, h * KV_LORA + c * LANES:h * KV_LORA + (c + 1) * LANES] = (acc_ref[hs, cs] * inv).astype(BF16)


def flash_attention(q_wide, k_wide):
    nq = SEQ // TQ
    nk = SEQ // TK
    rows_all = MLA_HEADS * TQ
    return pl.pallas_call(
        _flash_body,
        grid=(BATCH, nq, nk),
        in_specs=[
            pl.BlockSpec((TQ, MLA_HEADS * QK_WIDE), lambda b, qi, ki: (b * nq + qi, 0)),
            pl.BlockSpec((TK, QK_WIDE), lambda b, qi, ki: (b * nk + jnp.minimum(ki, qi // Q_PER_K), 0)),
        ],
        out_specs=pl.BlockSpec((TQ, MLA_HEADS * KV_LORA), lambda b, qi, ki: (b * nq + qi, 0)),
        out_shape=jax.ShapeDtypeStruct((N_PROMPT, MLA_HEADS * KV_LORA), BF16),
        scratch_shapes=[
            pltpu.VMEM((rows_all, QK_WIDE), BF16),
            pltpu.VMEM((rows_all, KV_LORA), F32),
            pltpu.VMEM((rows_all, LANES), F32),
            pltpu.VMEM((rows_all, LANES), F32),
        ],
        compiler_params=_cparams(("parallel", "parallel", "arbitrary")),
        name="flash_attention",
    )(q_wide, k_wide)


def _paged_body(pt_ref, q_ref, knew_ref, *rest):
    del pt_ref
    npg = PAGES_PER_STEP
    ckv_refs = rest[:npg]
    kr_refs = rest[npg:2 * npg]
    o_ref, acc_ref, m_ref, l_ref = rest[2 * npg:]
    step = pl.program_id(1)
    q = q_ref[0]
    q_lat = q[:, :KV_LORA]
    q_rope = q[:, KV_LORA:KV_LORA + QK_ROPE]
    nt = (((1,), (1,)), ((), ()))

    def online(parts, values):
        w = parts[0].shape[1]
        m_prev = m_ref[...]
        m_new = jnp.maximum(m_prev, jnp.max(functools.reduce(jnp.maximum, parts), axis=-1, keepdims=True))
        ps = [jnp.exp2((t - m_new[:, :w]) * EXP2_SCALE) for t in parts]
        alpha = jnp.exp2((m_prev - m_new) * EXP2_SCALE)
        l_ref[...] = alpha * l_ref[...] + jnp.sum(functools.reduce(jnp.add, ps), axis=-1, keepdims=True)
        pv = functools.reduce(jnp.add, [jnp.dot(t.astype(BF16), val, preferred_element_type=F32)
                                        for t, val in zip(ps, values)])
        for c in range(KV_LORA // LANES):
            cs = slice(c * LANES, (c + 1) * LANES)
            acc_ref[:, cs] = alpha * acc_ref[:, cs] + pv[:, cs]
        m_ref[...] = m_new

    @pl.when(step == 0)
    def _():
        m_ref[...] = jnp.full(m_ref.shape, -jnp.inf, F32)
        l_ref[...] = jnp.zeros(l_ref.shape, F32)
        acc_ref[...] = jnp.zeros(acc_ref.shape, F32)
        kn = knew_ref[0]
        s = lax.dot_general(q, kn, nt, preferred_element_type=F32)
        tok = lax.broadcasted_iota(jnp.int32, s.shape, 0) & (DEC_SEQ - 1)
        key = lax.broadcasted_iota(jnp.int32, s.shape, 1)
        online([jnp.where(key <= tok, s, -jnp.inf)], [kn[:, :KV_LORA]])

    vals = [r[0, 0].astype(BF16) for r in ckv_refs]
    parts = [
        lax.dot_general(q_lat, vals[i], nt, preferred_element_type=F32)
        + jnp.dot(q_rope, kr_refs[i][0, 0].astype(BF16), preferred_element_type=F32)
        for i in range(npg)
    ]
    online(parts, vals)

    @pl.when(step == pl.num_programs(1) - 1)
    def _():
        inv = 1.0 / l_ref[...]
        o_ref[0] = (acc_ref[...] * jnp.concatenate([inv] * (KV_LORA // LANES), axis=1)).astype(BF16)


def paged_attention(page_table, q_rows, k_new, ckv_pool, kr_pool_t, layer):
    npg = PAGES_PER_STEP
    rows = MLA_HEADS * DEC_SEQ

    def page_spec(shape, i):
        return pl.BlockSpec((1, 1) + shape, lambda b, p, pt: (layer, pt[b, p * npg + i], 0, 0))

    grid_spec = pltpu.PrefetchScalarGridSpec(
        num_scalar_prefetch=1,
        grid=(DEC_BATCH, N_PAGES // npg),
        in_specs=[
            pl.BlockSpec((1, rows, QK_WIDE), lambda b, p, pt: (b, 0, 0)),
            pl.BlockSpec((1, MLSTM_CHUNK_S, QK_WIDE), lambda b, p, pt: (b, 0, 0)),
        ] + [page_spec((PAGE_SIZE, KV_LORA), i) for i in range(npg)]
        + [page_spec((QK_ROPE, PAGE_SIZE), i) for i in range(npg)],
        out_specs=pl.BlockSpec((1, rows, KV_LORA), lambda b, p, pt: (b, 0, 0)),
        scratch_shapes=[
            pltpu.VMEM((rows, KV_LORA), F32),
            pltpu.VMEM((rows, LANES), F32),
            pltpu.VMEM((rows, LANES), F32),
        ],
    )
    return pl.pallas_call(
        _paged_body,
        grid_spec=grid_spec,
        out_shape=jax.ShapeDtypeStruct((DEC_BATCH, rows, KV_LORA), BF16),
        compiler_params=_cparams(("parallel", "arbitrary")),
        name="paged_attention",
    )(page_table, q_rows, k_new, *([ckv_pool] * npg), *([kr_pool_t] * npg))


def _mla_out_body(ol_ref, wv_ref, wo_ref, r_ref, out_ref, o_scr):
    for h in range(MLA_HEADS):
        oh = jnp.dot(ol_ref[:, h * KV_LORA:(h + 1) * KV_LORA], wv_ref[h], preferred_element_type=F32)
        o_scr[:, h * V_HEAD:(h + 1) * V_HEAD] = oh.astype(BF16)
    out_ref[...] = r_ref[...] + jnp.dot(o_scr[...], wo_ref[...], preferred_element_type=F32)


def mla_out(h, o_lat, wv, wo, *, row_block_offset):
    m = o_lat.shape[0]
    tm = TM_ROW
    return pl.pallas_call(
        _mla_out_body,
        grid=(m // tm,),
        in_specs=[
            pl.BlockSpec((tm, MLA_HEADS * KV_LORA), lambda i: (i, 0)),
            pl.BlockSpec((MLA_HEADS, KV_LORA, V_HEAD), lambda i: (0, 0, 0)),
            pl.BlockSpec((MLA_HEADS * V_HEAD, D_MODEL), lambda i: (0, 0)),
            pl.BlockSpec((tm, D_MODEL), lambda i: (i + row_block_offset, 0)),
        ],
        out_specs=pl.BlockSpec((tm, D_MODEL), lambda i: (i + row_block_offset, 0)),
        out_shape=jax.ShapeDtypeStruct(h.shape, F32),
        scratch_shapes=[pltpu.VMEM((tm, MLA_HEADS * V_HEAD), BF16)],
        input_output_aliases={3: 0},
        compiler_params=_cparams(("parallel",)),
        name="mla_out",
    )(o_lat, wv, wo, h)


def _router_body(h_ref, g_ref, w_ref, b_ref, u_ref, e_ref, gate_ref):
    u = _rms(h_ref[...], g_ref[...])
    u_ref[...] = u
    lg = jnp.dot(u.astype(BF16), w_ref[...], preferred_element_type=F32) + b_ref[...]
    lane = lax.broadcasted_iota(jnp.int32, lg.shape, 1)
    far = jnp.int32(LANES)
    lgm = jnp.where(lane < N_GROUPS, lg, -jnp.inf)
    mg = jnp.max(lgm, axis=-1, keepdims=True)
    pg_sel = 1.0 / jnp.sum(jnp.exp(lgm - mg), axis=-1, keepdims=True)
    g_sel = jnp.min(jnp.where(lgm == mg, lane, far), axis=-1, keepdims=True)
    lo = N_GROUPS + EXPERTS_PER_GROUP * g_sel
    in_group = jnp.logical_and(lane >= lo, lane < lo + EXPERTS_PER_GROUP)
    lem = jnp.where(in_group, lg, -jnp.inf)
    me = jnp.max(lem, axis=-1, keepdims=True)
    ee = jnp.exp(lem - me)
    pe = ee / jnp.sum(ee, axis=-1, keepdims=True)
    pe = jnp.where(in_group, pe, -1.0)
    p1 = jnp.max(pe, axis=-1, keepdims=True)
    i1 = jnp.min(jnp.where(pe == p1, lane, far), axis=-1, keepdims=True)
    pe2 = jnp.where(lane == i1, -1.0, pe)
    p2 = jnp.max(pe2, axis=-1, keepdims=True)
    i2 = jnp.min(jnp.where(pe2 == p2, lane, far), axis=-1, keepdims=True)
    norm = pg_sel / (p1 + p2)
    e_ref[...] = jnp.where(lane == 0, i1 - N_GROUPS, jnp.where(lane == 1, i2 - N_GROUPS, 0))
    gate_ref[...] = jnp.where(lane == 0, p1 * norm, jnp.where(lane == 1, p2 * norm, 0.0))


def router(h, g, w, b):
    n, d = h.shape
    tm = TM_ROW
    row = lambda i: (i, 0)
    fixed = lambda i: (0, 0)
    return pl.pallas_call(
        _router_body,
        grid=(n // tm,),
        in_specs=[
            pl.BlockSpec((tm, d), row),
            pl.BlockSpec((1, d), fixed),
            pl.BlockSpec((d, LANES), fixed),
            pl.BlockSpec((1, LANES), fixed),
        ],
        out_specs=[pl.BlockSpec((tm, d), row), pl.BlockSpec((tm, LANES), row), pl.BlockSpec((tm, LANES), row)],
        out_shape=[
            jax.ShapeDtypeStruct((n, d), F32),
            jax.ShapeDtypeStruct((n, LANES), jnp.int32),
            jax.ShapeDtypeStruct((n, LANES), F32),
        ],
        compiler_params=_cparams(("parallel",)),
        name="router",
    )(h, g, w, b)


def _gather_body(idx_ref, live_ref, src_ref, out_ref, sem):
    blk = pl.program_id(0)
    base = blk * GATHER_ROWS

    def row_copy(r, src_row):
        return pltpu.make_async_copy(src_ref.at[pl.ds(src_row, 1)], out_ref.at[pl.ds(r, 1)], sem)

    def start(r, carry):
        row_copy(r, idx_ref[base + r]).start()
        return carry

    def wait(r, carry):
        row_copy(r, 0).wait()
        return carry

    @pl.when(blk < live_ref[0])
    def _():
        lax.fori_loop(0, GATHER_ROWS, start, 0)
        lax.fori_loop(0, GATHER_ROWS, wait, 0)

    @pl.when(blk >= live_ref[0])
    def _():
        out_ref[...] = jnp.zeros(out_ref.shape, out_ref.dtype)


def gather_rows(src, idx, live_blocks):
    n_out = idx.shape[0]
    width = src.shape[1]
    grid_spec = pltpu.PrefetchScalarGridSpec(
        num_scalar_prefetch=2,
        grid=(n_out // GATHER_ROWS,),
        in_specs=[pl.BlockSpec(memory_space=pl.ANY)],
        out_specs=pl.BlockSpec((GATHER_ROWS, width), lambda i, idx, live: (i, 0)),
        scratch_shapes=[pltpu.SemaphoreType.DMA(())],
    )
    return pl.pallas_call(
        _gather_body,
        grid_spec=grid_spec,
        out_shape=jax.ShapeDtypeStruct((n_out, width), src.dtype),
        compiler_params=_cparams(("arbitrary",)),
        name="gather_rows",
    )(idx, live_blocks, src)


def _experts_body(be_ref, nu_ref, x_ref, w1_ref, w3_ref, w2_ref, y_ref, w1b, w3b, w2b):
    b = pl.program_id(0)
    changed = jnp.logical_or(b == 0, be_ref[b] != be_ref[jnp.maximum(b - 1, 0)])

    @pl.when(changed)
    def _():
        w1b[...] = w1_ref[0, 0].astype(BF16)
        w3b[...] = w3_ref[0, 0].astype(BF16)
        w2b[...] = w2_ref[0, 0].astype(BF16)

    @pl.when(b < nu_ref[0])
    def _():
        x = x_ref[...].astype(BF16)
        a = jnp.dot(x, w1b[...], preferred_element_type=F32)
        g = jnp.dot(x, w3b[...], preferred_element_type=F32)
        mid = (a * _sigmoid(a) * g).astype(BF16)
        y_ref[...] = jnp.dot(mid, w2b[...], preferred_element_type=F32)

    @pl.when(b >= nu_ref[0])
    def _():
        y_ref[...] = jnp.zeros(y_ref.shape, F32)


def experts_ffn(block_expert, n_used, x_pad, w1, w3, w2, layer):
    n_slots, d = x_pad.shape
    n_blocks = n_slots // MOE_BM
    grid_spec = pltpu.PrefetchScalarGridSpec(
        num_scalar_prefetch=2,
        grid=(n_blocks,),
        in_specs=[
            pl.BlockSpec((MOE_BM, d), lambda b, be, nu: (b, 0)),
            pl.BlockSpec((1, 1, d, D_EXPERT), lambda b, be, nu: (layer, be[b], 0, 0)),
            pl.BlockSpec((1, 1, d, D_EXPERT), lambda b, be, nu: (layer, be[b], 0, 0)),
            pl.BlockSpec((1, 1, D_EXPERT, d), lambda b, be, nu: (layer, be[b], 0, 0)),
        ],
        out_specs=pl.BlockSpec((MOE_BM, d), lambda b, be, nu: (b, 0)),
        scratch_shapes=[
            pltpu.VMEM((d, D_EXPERT), BF16),
            pltpu.VMEM((d, D_EXPERT), BF16),
            pltpu.VMEM((D_EXPERT, d), BF16),
        ],
    )
    return pl.pallas_call(
        _experts_body,
        grid_spec=grid_spec,
        out_shape=jax.ShapeDtypeStruct((n_slots, d), F32),
        compiler_params=_cparams(("arbitrary",)),
        name="experts_ffn",
    )(block_expert, n_used, x_pad, w1, w3, w2)


def moe_dispatch(experts):
    a = N_TOK * TOP_K
    n_blocks = a // MOE_BM + N_EXPERTS
    flat_e = experts.reshape(a)
    order = jnp.argsort(flat_e)
    sorted_e = flat_e[order]
    counts = jnp.bincount(flat_e, length=N_EXPERTS)
    padded = (counts + MOE_BM - 1) // MOE_BM * MOE_BM
    start = jnp.cumsum(counts) - counts
    pad_end = jnp.cumsum(padded)
    pad_start = pad_end - padded
    slot_sorted = (pad_start[sorted_e] + jnp.arange(a) - start[sorted_e]).astype(jnp.int32)
    slot = jnp.zeros((a,), jnp.int32).at[order].set(slot_sorted)
    n_slots = n_blocks * MOE_BM
    token_of_slot = (jnp.arange(n_slots, dtype=jnp.int32) % N_TOK).at[slot].set(
        jnp.arange(a, dtype=jnp.int32) // TOP_K)
    block_expert = jnp.minimum(
        jnp.searchsorted(pad_end, jnp.arange(n_blocks) * MOE_BM, side='right'), N_EXPERTS - 1).astype(jnp.int32)
    n_used = (pad_end[-1:] // MOE_BM).astype(jnp.int32)
    return slot, token_of_slot, block_expert, n_used


def _ple_body(h_ref, y0_ref, y1_ref, gate_ref, g_ref, wg_ref, p_ref, wp_ref, gf_ref, out_ref, *, final):
    gt = _round_bf16(gate_ref[...])
    h2 = h_ref[...] + (gt[:, 0:1] * _round_bf16(y0_ref[...]) + gt[:, 1:2] * _round_bf16(y1_ref[...]))
    xn = _rms(h2, g_ref[...]).astype(BF16)
    gate = _sigmoid(jnp.dot(xn, wg_ref[...], preferred_element_type=F32))
    emb = jnp.dot(p_ref[...].astype(BF16), wp_ref[...], preferred_element_type=F32)
    h3 = h2 + gate * emb
    out_ref[...] = _rms(h3, gf_ref[...]) if final else h3


def ple(h, y_kmajor, gates, g, wg, p, wp, g_final, *, final):
    n, d = h.shape
    tm = TM_ROW
    row = lambda i: (i, 0)
    fixed = lambda i: (0, 0)
    return pl.pallas_call(
        functools.partial(_ple_body, final=final),
        grid=(n // tm,),
        in_specs=[
            pl.BlockSpec((tm, d), row),
            pl.BlockSpec((tm, d), row),
            pl.BlockSpec((tm, d), lambda i: (i + n // tm, 0)),
            pl.BlockSpec((tm, LANES), row),
            pl.BlockSpec((1, d), fixed),
            pl.BlockSpec((d, d), fixed),
            pl.BlockSpec((tm, PLE_DIM), row),
            pl.BlockSpec((PLE_DIM, d), fixed),
            pl.BlockSpec((1, d), fixed),
        ],
        out_specs=pl.BlockSpec((tm, d), row),
        out_shape=jax.ShapeDtypeStruct((n, d), F32),
        compiler_params=_cparams(("parallel",)),
        name="ple",
    )(h, y_kmajor, y_kmajor, gates, g, wg, p, wp, g_final)


def _pad_cols(w, width):
    return jnp.pad(w, ((0, 0), (0, width - w.shape[1])))


def _rope_tables():
    pos = jnp.concatenate([jnp.tile(jnp.arange(SEQ), BATCH), jnp.tile(PAST_LEN + jnp.arange(DEC_SEQ), DEC_BATCH)])
    inv_freq = ROPE_THETA ** (-jnp.arange(0, QK_ROPE, 2, dtype=F32) / QK_ROPE)
    ang = pos.astype(F32)[:, None] * inv_freq[None, :]
    cos, sin = jnp.cos(ang), jnp.sin(ang)
    cc = jnp.concatenate([cos, cos], axis=1)
    ss = jnp.concatenate([-sin, sin], axis=1)
    return jnp.tile(cc, (1, LANES // QK_ROPE)), jnp.tile(ss, (1, LANES // QK_ROPE))


def _swap_halves(w):
    half = w.shape[-1] // 2
    return jnp.concatenate([w[..., half:], w[..., :half]], axis=-1)


def _mlstm_layer(h, j, ln, w_in, b_if, g_head, w_out, state_c, state_n, state_m, c_s_other):
    w_main = w_in[:, :MLSTM_MAIN].astype(BF16)
    w_gate = _pad_cols(w_in[:, MLSTM_MAIN:], LANES).astype(BF16)
    qkv, o_gate, gates = norm_proj(h, ln.reshape(1, D_MODEL), w_main, w_gate)
    bias = _pad_cols(b_if.reshape(1, 2 * MLSTM_HEADS), LANES)
    gh = g_head.reshape(1, D_MODEL)

    zc = jnp.zeros((1, BATCH, MLSTM_HEADS, MLSTM_DK, MLSTM_DV), F32)
    zn = jnp.zeros((BATCH, MLSTM_HEADS, MLSTM_DK), F32)
    zm = jnp.zeros((BATCH, 1, LANES), F32)
    gated_p, c_p, n_p, m_p = mlstm_scan(qkv, o_gate, gates, bias, gh, zc, zn, zm, n_seq=BATCH,
                                        n_chunks=SEQ // MLSTM_CHUNK_P, chunk=MLSTM_CHUNK_P, valid=MLSTM_CHUNK_P)

    def pad_rows(a):
        a = a[N_PROMPT:].reshape(DEC_BATCH, DEC_SEQ, a.shape[1])
        return jnp.pad(a, ((0, 0), (0, MLSTM_CHUNK_S - DEC_SEQ), (0, 0))).reshape(-1, a.shape[2])

    m0 = _pad_cols(state_m[j], LANES).reshape(DEC_BATCH, 1, LANES)
    gated_s, c_s, n_s, m_s = mlstm_scan(pad_rows(qkv), pad_rows(o_gate), pad_rows(gates), bias, gh, state_c,
                                        state_n[j], m0, n_seq=DEC_BATCH, n_chunks=1, chunk=MLSTM_CHUNK_S,
                                        valid=DEC_SEQ, layer=j, c_other=c_s_other)
    gated_s = gated_s.reshape(DEC_BATCH, MLSTM_CHUNK_S, D_MODEL)[:, :DEC_SEQ].reshape(N_SAMPLE, D_MODEL)

    w_o = w_out.astype(BF16)
    h = linear_res(h, gated_p, w_o, row_block_offset=0)
    h = linear_res(h, gated_s, w_o, row_block_offset=N_PROMPT // TM)
    state_p = (c_p[0], n_p, m_p[:, 0, :MLSTM_HEADS])
    state_s = (n_s, m_s[:, 0, :MLSTM_HEADS])
    return h, state_p, state_s, c_s


def _mla_layer(h, j, ln, w_a, g_q, w_qb, g_kv, w_kvb, w_o, cc, ss, cache_ckv, cache_kr, page_table):
    w_kr = w_a[:, Q_LORA + KV_LORA:]
    w_a_ext = jnp.concatenate([w_a, _swap_halves(w_kr)], axis=1).astype(BF16)
    cq, ckv, kr, k_wide = mla_a(h, ln.reshape(1, D_MODEL), w_a_ext, g_q.reshape(1, Q_LORA),
                                g_kv.reshape(1, KV_LORA), cc, ss)

    wq = w_qb.reshape(Q_LORA, MLA_HEADS, QK_NOPE + QK_ROPE)
    w_nope = wq[:, :, :QK_NOPE].reshape(Q_LORA, MLA_HEADS * QK_NOPE).astype(BF16)
    w_rope = wq[:, :, QK_NOPE:]
    pad = ((0, 0), (0, 0), (0, LANES - QK_ROPE))
    w_r = jnp.pad(w_rope, pad).reshape(Q_LORA, MLA_HEADS * LANES).astype(BF16)
    w_rs = jnp.pad(_swap_halves(w_rope), pad).reshape(Q_LORA, MLA_HEADS * LANES).astype(BF16)
    wkv = w_kvb.reshape(KV_LORA, MLA_HEADS, QK_NOPE + V_HEAD)
    wk_t = jnp.transpose(wkv[:, :, :QK_NOPE], (1, 2, 0)).astype(BF16)
    wv = jnp.transpose(wkv[:, :, QK_NOPE:], (1, 0, 2)).astype(BF16)
    q_wide = mla_q(cq, w_nope, w_r, w_rs, wk_t, cc, ss)

    o_lat_p = flash_attention(q_wide, k_wide)

    q_s = q_wide[N_PROMPT:].reshape(DEC_BATCH, DEC_SEQ, MLA_HEADS, QK_WIDE)
    q_s = jnp.transpose(q_s, (0, 2, 1, 3)).reshape(DEC_BATCH, MLA_HEADS * DEC_SEQ, QK_WIDE)
    k_new = jnp.pad(k_wide[N_PROMPT:].reshape(DEC_BATCH, DEC_SEQ, QK_WIDE),
                    ((0, 0), (0, MLSTM_CHUNK_S - DEC_SEQ), (0, 0)))
    o_s = paged_attention(page_table, q_s, k_new, cache_ckv, jnp.swapaxes(cache_kr, 2, 3), j)
    o_lat_s = jnp.transpose(o_s.reshape(DEC_BATCH, MLA_HEADS, DEC_SEQ, KV_LORA), (0, 2, 1, 3))
    o_lat_s = o_lat_s.reshape(N_SAMPLE, MLA_HEADS * KV_LORA)

    wo = w_o.astype(BF16)
    h = mla_out(h, o_lat_p, wv, wo, row_block_offset=0)
    h = mla_out(h, o_lat_s, wv, wo, row_block_offset=N_PROMPT // TM_ROW)
    return h, ckv, kr


def _moe_ple_layer(h, i, ln_ffn, w_rg, b_rg, w_re, b_re, w1, w3, w2, p_all, ple_g, ple_wg, ple_wp, ln_final):
    w_r = _pad_cols(jnp.concatenate([w_rg, w_re], axis=1), LANES).astype(BF16)
    b_r = _pad_cols(jnp.concatenate([b_rg, b_re]).reshape(1, -1), LANES)
    u, e_idx, gates = router(h, ln_ffn.reshape(1, D_MODEL), w_r, b_r)
    slot, token_of_slot, block_expert, n_used = moe_dispatch(e_idx[:, :TOP_K])
    x_pad = gather_rows(u, token_of_slot, n_used * (MOE_BM // GATHER_ROWS))
    y_pad = experts_ffn(block_expert, n_used, x_pad, w1, w3, w2, i)
    all_blocks = jnp.full((1,), N_TOK * TOP_K // GATHER_ROWS, jnp.int32)
    y_kmajor = gather_rows(y_pad, slot.reshape(N_TOK, TOP_K).T.reshape(-1), all_blocks)
    return ple(h, y_kmajor, gates, ple_g.reshape(1, D_MODEL), ple_wg.astype(BF16), p_all, ple_wp.astype(BF16),
               ln_final.reshape(1, D_MODEL), final=(i == DEPTH - 1))


def kernel(x_prompt, x_sample, cache_mla_ckv, cache_mla_krope, state_mlstm_C, state_mlstm_n, state_mlstm_m,
           page_table, p_prompt, p_sample, ln_mix, ln_ffn, ln_final, mlstm_w_in, mlstm_b_if, mlstm_g_head,
           mlstm_w_out, mla_w_a, mla_g_q, mla_w_qb, mla_g_kv, mla_w_kvb, mla_w_o, moe_w_rg, moe_b_rg, moe_w_re,
           moe_b_re, moe_w1, moe_w3, moe_w2, ple_g, ple_w_gate, ple_w_proj):
    h = jnp.concatenate([x_prompt.reshape(N_PROMPT, D_MODEL), x_sample.reshape(N_SAMPLE, D_MODEL)], axis=0)
    cc, ss = _rope_tables()
    ckv_rows, kr_rows, states_p, states_s = [], [], [], []
    c_s = None
    for i in range(DEPTH):
        j = i // 2
        if i % 2 == 0:
            h, st_p, st_s, c_s = _mlstm_layer(h, j, ln_mix[i], mlstm_w_in[j], mlstm_b_if[j], mlstm_g_head[j],
                                              mlstm_w_out[j], state_mlstm_C, state_mlstm_n, state_mlstm_m, c_s)
            states_p.append(st_p)
            states_s.append(st_s)
        else:
            h, ckv, kr = _mla_layer(h, j, ln_mix[i], mla_w_a[j], mla_g_q[j], mla_w_qb[j], mla_g_kv[j],
                                    mla_w_kvb[j], mla_w_o[j], cc, ss, cache_mla_ckv, cache_mla_krope, page_table)
            ckv_rows.append(ckv)
            kr_rows.append(kr)
        p_all = jnp.concatenate([p_prompt[i].reshape(N_PROMPT, PLE_DIM), p_sample[i].reshape(N_SAMPLE, PLE_DIM)])
        h = _moe_ple_layer(h, i, ln_ffn[i], moe_w_rg[i], moe_b_rg[i], moe_w_re[i], moe_b_re[i], moe_w1, moe_w3,
                           moe_w2, p_all, ple_g[i], ple_w_gate[i], ple_w_proj[i], ln_final)

    ckv_all = jnp.stack(ckv_rows)
    kr_all = jnp.stack(kr_rows)

    def split(a, width):
        return (a[:, :N_PROMPT].reshape(-1, BATCH, SEQ, width), a[:, N_PROMPT:].reshape(-1, DEC_BATCH, DEC_SEQ, width))

    ckv_p, ckv_s = split(ckv_all, KV_LORA)
    kr_p, kr_s = split(kr_all, QK_ROPE)
    c_p, n_p, m_p = (jnp.stack(t) for t in zip(*states_p))
    n_s, m_s = (jnp.stack(t) for t in zip(*states_s))
    y_prompt = h[:N_PROMPT].reshape(BATCH, SEQ, D_MODEL)
    y_sample = h[N_PROMPT:].reshape(DEC_BATCH, DEC_SEQ, D_MODEL)
    return (y_prompt, y_sample, ckv_p, kr_p, c_p, n_p, m_p, ckv_s, kr_s, c_s, n_s, m_s)
```
